```python
import math
import jax, jax.numpy as jnp
from jax import lax
import numpy as np

D_MODEL = 1024
BATCH = 8
SEQ = 2048
DEPTH = 4
DEC_BATCH = 8
DEC_SEQ = 16
PAST_LEN = 4096

CHUNK = 64
N_META = 16
N_EVEN = (DEPTH + 1) // 2
N_ODD = DEPTH // 2
A_WIDTH = D_MODEL
A_CONV_W = 3
SB_HEADS = 16
SB_HEAD_DIM = 64
SB_WIDTH = SB_HEADS * SB_HEAD_DIM
Q_BLOCK = 128
EVEN_IN = 4 * A_WIDTH + 4 * SB_WIDTH
EVEN_SPLITS = (A_WIDTH, 2 * A_WIDTH, 3 * A_WIDTH, 4 * A_WIDTH,
               4 * A_WIDTH + SB_WIDTH, 4 * A_WIDTH + 2 * SB_WIDTH, 4 * A_WIDTH + 3 * SB_WIDTH)
EVEN_MIX = A_WIDTH + SB_WIDTH
C_WIDTH = 2 * D_MODEL
C_CONV_W = 31
ODD_IN = 3 * C_WIDTH
ALPHA = (2 * DEPTH) ** 0.25
INIT_BETA = (8 * DEPTH) ** -0.25
LN_EPS = 1e-5

kernel_name = "streaming_shortconv_stickbreak_conformer_trunk"


def layer_norm(x, g, b):
    xf = x.astype(jnp.float32)
    mu = jnp.mean(xf, axis=-1, keepdims=True)
    var = jnp.mean(jnp.square(xf - mu), axis=-1, keepdims=True)
    y = (xf - mu) * lax.rsqrt(var + LN_EPS) * g.astype(jnp.float32) + b.astype(jnp.float32)
    return y.astype(x.dtype)


def causal_dwconv(u, buf, w):
    width, ch = w.shape
    ext = jnp.concatenate([buf.astype(u.dtype), u], axis=1)
    y = lax.conv_general_dilated(ext, w[:, None, :].astype(u.dtype), window_strides=(1,),
                                 padding='VALID', dimension_numbers=('NWC', 'WIO', 'NWC'),
                                 feature_group_count=ch)
    return y, ext[:, ext.shape[1] - (width - 1):]


def stick_breaking(q, k, v, q_pos, k_pos):
    z = jnp.einsum('bhqd,bhkd->bhqk', q, k).astype(jnp.float32) * (1.0 / math.sqrt(SB_HEAD_DIM))
    mask = k_pos[None, :] < q_pos[:, None]
    log_beta = jax.nn.log_sigmoid(z)
    log_1m_beta = jnp.where(mask, log_beta - z, 0.0)
    between = lax.cumsum(log_1m_beta, axis=3, reverse=True) - log_1m_beta
    a = jnp.where(mask, jnp.exp(log_beta + between), 0.0)
    return jnp.einsum('bhqk,bhkd->bhqd', a.astype(v.dtype), v)


def stick_breaking_prompt(q, k, v):
    b, h, length, dh = q.shape
    n_blk = -(-length // Q_BLOCK)
    qp = jnp.pad(q, ((0, 0), (0, 0), (0, n_blk * Q_BLOCK - length), (0, 0)))
    q_blocks = qp.reshape(b, h, n_blk, Q_BLOCK, dh).transpose(2, 0, 1, 3, 4)
    k_pos = jnp.arange(length, dtype=jnp.int32)

    def one_block(args):
        qb, start = args
        return stick_breaking(qb, k, v, start + jnp.arange(Q_BLOCK, dtype=jnp.int32), k_pos)

    starts = jnp.arange(n_blk, dtype=jnp.int32) * Q_BLOCK
    out = lax.map(one_block, (q_blocks, starts))
    out = out.transpose(1, 2, 0, 3, 4).reshape(b, h, n_blk * Q_BLOCK, dh)
    return out[:, :, :length]


def split_heads(t):
    b, n, _ = t.shape
    return t.reshape(b, n, SB_HEADS, SB_HEAD_DIM).transpose(0, 2, 1, 3)


def even_mixer(x, conv_buf, k_past, v_past, w_in, conv_w, w_out):
    b, n, _ = x.shape
    h, gate_b, gate_c, z_a, q, k, v, z_b = jnp.split(x @ w_in, EVEN_SPLITS, axis=-1)
    conv_out, new_buf = causal_dwconv(gate_c * h, conv_buf, conv_w)
    y_a = gate_b * conv_out * jax.nn.silu(z_a)
    qh, kh, vh = split_heads(q), split_heads(k), split_heads(v)
    if k_past is None:
        o = stick_breaking_prompt(qh, kh, vh)
    else:
        p = k_past.shape[2]
        o = stick_breaking(qh, jnp.concatenate([k_past.astype(kh.dtype), kh], axis=2),
                           jnp.concatenate([v_past.astype(vh.dtype), vh], axis=2),
                           p + jnp.arange(n, dtype=jnp.int32), jnp.arange(p + n, dtype=jnp.int32))
    y_b = o.transpose(0, 2, 1, 3).reshape(b, n, SB_WIDTH) * jax.nn.silu(z_b)
    out = jnp.concatenate([y_a, y_b], axis=-1) @ w_out
    return out, new_buf, kh, vh


def odd_mixer(x, conv_buf, w_in, conv_w, conv_b, ln_g, ln_b, w_out):
    a, g, z_c = jnp.split(x @ w_in, 3, axis=-1)
    u = a * jax.nn.sigmoid(g)
    c, new_buf = causal_dwconv(u, conv_buf, conv_w)
    c = c + conv_b
    y = jax.nn.silu(layer_norm(c, ln_g, ln_b)) * jax.nn.silu(z_c)
    return y @ w_out, new_buf


def run_trunk(x, conv_a_bufs, conv_c_bufs, k_pasts, v_pasts, w_in_even, conv_a_w, w_out_even,
              w_in_odd, conv_c_w, conv_c_b, ln_c_g, ln_c_b, w_out_odd, post_ln_g, post_ln_b):
    new_k, new_v, new_a, new_c = [], [], [], []
    for layer in range(DEPTH):
        i = layer // 2
        if layer % 2 == 0:
            kp = None if k_pasts is None else k_pasts[i]
            vp = None if v_pasts is None else v_pasts[i]
            out, buf, kh, vh = even_mixer(x, conv_a_bufs[i], kp, vp, w_in_even[i], conv_a_w[i], w_out_even[i])
            new_k.append(kh)
            new_v.append(vh)
            new_a.append(buf)
        else:
            out, buf = odd_mixer(x, conv_c_bufs[i], w_in_odd[i], conv_c_w[i], conv_c_b[i],
                                 ln_c_g[i], ln_c_b[i], w_out_odd[i])
            new_c.append(buf)
        x = layer_norm(ALPHA * x + out, post_ln_g[layer], post_ln_b[layer])
    return x, jnp.stack(new_k), jnp.stack(new_v), jnp.stack(new_a), jnp.stack(new_c)


def setup_inputs(seed: int = 0) -> dict:
    key = jax.random.key(seed)
    ks = jax.random.split(key, 20)
    f32 = jnp.float32

    def nrm(k, shape, scale):
        return jax.random.normal(k, shape, f32) * scale

    return {
        "x_prompt": nrm(ks[0], (BATCH, SEQ, D_MODEL), 1.0),
        "x_sample": nrm(ks[1], (DEC_BATCH, DEC_SEQ, D_MODEL), 1.0),
        "cache_sb_k": nrm(ks[2], (N_EVEN, DEC_BATCH, SB_HEADS, N_META + PAST_LEN, SB_HEAD_DIM), 1.0),
        "cache_sb_v": nrm(ks[3], (N_EVEN, DEC_BATCH, SB_HEADS, N_META + PAST_LEN, SB_HEAD_DIM), 1.0),
        "state_conv_a": nrm(ks[4], (N_EVEN, DEC_BATCH, A_CONV_W - 1, A_WIDTH), 1.0),
        "state_conv_c": nrm(ks[5], (N_ODD, DEC_BATCH, C_CONV_W - 1, C_WIDTH), 1.0),
        "meta_tokens": nrm(ks[6], (N_META, D_MODEL), 1.0),
        "w_in_even": nrm(ks[7], (N_EVEN, D_MODEL, EVEN_IN), D_MODEL ** -0.5),
        "conv_a_w": nrm(ks[8], (N_EVEN, A_CONV_W, A_WIDTH), A_CONV_W ** -0.5),
        "w_out_even": nrm(ks[9], (N_EVEN, EVEN_MIX, D_MODEL), INIT_BETA * EVEN_MIX ** -0.5),
        "w_in_odd": nrm(ks[10], (N_ODD, D_MODEL, ODD_IN), D_MODEL ** -0.5),
        "conv_c_w": nrm(ks[11], (N_ODD, C_CONV_W, C_WIDTH), C_CONV_W ** -0.5),
        "conv_c_b": nrm(ks[12], (N_ODD, C_WIDTH), 0.02),
        "ln_c_g": 1.0 + nrm(ks[13], (N_ODD, C_WIDTH), 0.02),
        "ln_c_b": nrm(ks[14], (N_ODD, C_WIDTH), 0.02),
        "w_out_odd": nrm(ks[15], (N_ODD, C_WIDTH, D_MODEL), INIT_BETA * C_WIDTH ** -0.5),
        "post_ln_g": 1.0 + nrm(ks[16], (DEPTH, D_MODEL), 0.02),
        "post_ln_b": nrm(ks[17], (DEPTH, D_MODEL), 0.02),
    }


def reference(x_prompt, x_sample, cache_sb_k, cache_sb_v, state_conv_a, state_conv_c, meta_tokens,
              w_in_even, conv_a_w, w_out_even, w_in_odd, conv_c_w, conv_c_b, ln_c_g, ln_c_b,
              w_out_odd, post_ln_g, post_ln_b):
    weights = (w_in_even, conv_a_w, w_out_even, w_in_odd, conv_c_w, conv_c_b, ln_c_g, ln_c_b,
               w_out_odd, post_ln_g, post_ln_b)
    b_p = x_prompt.shape[0]
    meta = jnp.broadcast_to(meta_tokens.astype(x_prompt.dtype)[None], (b_p, N_META, D_MODEL))
    xp = jnp.concatenate([meta, x_prompt], axis=1)
    zeros_a = jnp.zeros((N_EVEN, b_p, A_CONV_W - 1, A_WIDTH), x_prompt.dtype)
    zeros_c = jnp.zeros((N_ODD, b_p, C_CONV_W - 1, C_WIDTH), x_prompt.dtype)
    h_p, k_p, v_p, a_p, c_p = run_trunk(xp, zeros_a, zeros_c, None, None, *weights)
    y_prompt = h_p[:, N_META:]
    y_sample, k_s, v_s, a_s, c_s = run_trunk(x_sample, state_conv_a, state_conv_c,
                                             cache_sb_k, cache_sb_v, *weights)
    return (y_prompt, y_sample, k_p, v_p, a_p, c_p, k_s, v_s, a_s, c_s)
```

```python
import functools
import math

import jax
import jax.numpy as jnp
from jax import lax
from jax.experimental import pallas as pl
from jax.experimental.pallas import tpu as pltpu

F32 = jnp.float32
BF16 = jnp.bfloat16

D_MODEL = 1024
DEPTH = 4
N_META = 16
HEADS = 16
HEAD_DIM = 64
A_WIDTH = D_MODEL
A_CONV_W = 3
C_WIDTH = 2 * D_MODEL
C_CONV_W = 31
EVEN_IN = 8 * D_MODEL
ODD_IN = 3 * C_WIDTH
ALPHA = (2 * DEPTH) ** 0.25
LN_EPS = 1e-5
Q_SCALE = 1.0 / math.sqrt(HEAD_DIM)

LANES = 128
KEY_BLOCK = 128
VMEM_LIMIT = 56 * 1024 * 1024


def _params(n_axes):
    return pltpu.CompilerParams(dimension_semantics=("arbitrary",) * n_axes,
                                vmem_limit_bytes=VMEM_LIMIT)


def _proj_kernel(x_ref, w_ref, o_ref, xb_ref):
    @pl.when(pl.program_id(1) == 0)
    def _():
        xb_ref[...] = x_ref[...].astype(BF16)

    o_ref[...] = jnp.dot(xb_ref[...], w_ref[...], preferred_element_type=F32)


def _proj(x2d, w_bf16, tm, tn, name):
    m, k = x2d.shape
    n = w_bf16.shape[1]
    return pl.pallas_call(
        _proj_kernel,
        grid=(m // tm, n // tn),
        in_specs=[pl.BlockSpec((tm, k), lambda i, j: (i, 0)),
                  pl.BlockSpec((k, tn), lambda i, j: (0, j))],
        out_specs=pl.BlockSpec((tm, tn), lambda i, j: (i, j)),
        out_shape=jax.ShapeDtypeStruct((m, n), F32),
        scratch_shapes=[pltpu.VMEM((tm, k), BF16)],
        compiler_params=_params(2),
        name=name,
    )(x2d, w_bf16)


def _suffix_sum_matrix():
    r = lax.broadcasted_iota(jnp.int32, (2 * KEY_BLOCK, 2 * KEY_BLOCK), 0) & (KEY_BLOCK - 1)
    c = lax.broadcasted_iota(jnp.int32, (2 * KEY_BLOCK, 2 * KEY_BLOCK), 1)
    return jnp.where((c >= KEY_BLOCK) | (r > c), 1.0, 0.0).astype(BF16)


def _sb_block(q, kb, vb, m2, later, acc, mask):
    z = lax.dot_general(q, kb, (((1,), (1,)), ((), ())), preferred_element_type=F32)
    lse = jnp.log1p(jnp.exp(-jnp.abs(z)))
    log_beta = jnp.minimum(z, 0.0) - lse
    log_1m = log_beta - z
    if mask is not None:
        log_1m = jnp.where(mask, log_1m, 0.0)
    hi = log_1m.astype(BF16)
    lo = (log_1m - hi.astype(F32)).astype(BF16)
    c = jnp.dot(jnp.concatenate([hi, lo], axis=1), m2, preferred_element_type=F32)
    a = jnp.exp(log_beta + c[:, :KEY_BLOCK] + later)
    if mask is not None:
        a = jnp.where(mask, a, 0.0)
    acc = acc + jnp.dot(a.astype(BF16), vb, preferred_element_type=F32)
    return later + c[:, KEY_BLOCK:], acc


def _attn_prompt_kernel(q_ref, k_ref, v_ref, o_ref, nk_ref, nv_ref, qs, ks, vs, *, seq):
    n_full = seq // KEY_BLOCK
    tail = seq - n_full * KEY_BLOCK
    padded = (n_full + 1) * KEY_BLOCK
    m2 = _suffix_sum_matrix()
    lane = lax.broadcasted_iota(jnp.int32, (seq, LANES), 1)
    first = lane < HEAD_DIM

    k = k_ref[0]
    v = v_ref[0]
    q = q_ref[0] * Q_SCALE
    nk_ref[0, 0] = k[:, :HEAD_DIM]
    nk_ref[0, 1] = k[:, HEAD_DIM:]
    nv_ref[0, 0] = v[:, :HEAD_DIM]
    nv_ref[0, 1] = v[:, HEAD_DIM:]
    ks[0:seq, :] = k.astype(BF16)
    qs[0] = jnp.where(first, q, 0.0).astype(BF16)
    qs[1] = jnp.where(first, 0.0, q).astype(BF16)
    vs[0, 0:seq, :] = jnp.where(first, v, 0.0).astype(BF16)
    vs[1, 0:seq, :] = jnp.where(first, 0.0, v).astype(BF16)
    zpad = jnp.zeros((padded - seq, LANES), BF16)
    ks[seq:padded, :] = zpad
    vs[0, seq:padded, :] = zpad
    vs[1, seq:padded, :] = zpad

    def q_tile(base, rows, n_before):
        row = lax.broadcasted_iota(jnp.int32, (rows, KEY_BLOCK), 0)
        col = lax.broadcasted_iota(jnp.int32, (rows, KEY_BLOCK), 1)
        mask = col < row
        q0 = qs[0, pl.ds(base, rows), :]
        q1 = qs[1, pl.ds(base, rows), :]
        kd = ks[pl.ds(base, KEY_BLOCK), :]
        zero = jnp.zeros((rows, KEY_BLOCK), F32)
        l0, acc = _sb_block(q0, kd, vs[0, pl.ds(base, KEY_BLOCK), :], m2, zero, zero, mask)
        l1, acc = _sb_block(q1, kd, vs[1, pl.ds(base, KEY_BLOCK), :], m2, zero, acc, mask)

        def body(i, carry):
            l0, l1, acc = carry
            kb0 = pl.multiple_of((n_before - 1 - i) * KEY_BLOCK, KEY_BLOCK)
            kb = ks[pl.ds(kb0, KEY_BLOCK), :]
            l0, acc = _sb_block(q0, kb, vs[0, pl.ds(kb0, KEY_BLOCK), :], m2, l0, acc, None)
            l1, acc = _sb_block(q1, kb, vs[1, pl.ds(kb0, KEY_BLOCK), :], m2, l1, acc, None)
            return l0, l1, acc

        _, _, acc = lax.fori_loop(0, n_before, body, (l0, l1, acc))
        o_ref[0, pl.ds(base, rows), :] = acc

    def full_tile(m, carry):
        q_tile(pl.multiple_of(m * KEY_BLOCK, KEY_BLOCK), KEY_BLOCK, m)
        return carry

    lax.fori_loop(0, n_full, full_tile, 0)
    if tail:
        q_tile(n_full * KEY_BLOCK, tail, n_full)


def _attn_prompt(p3, name):
    b, t, _ = p3.shape
    padded = (t // KEY_BLOCK + 1) * KEY_BLOCK
    col0 = 4 * A_WIDTH // LANES
    per = D_MODEL // LANES

    def col_spec(which):
        return pl.BlockSpec((1, t, LANES), lambda i, j: (i, 0, col0 + which * per + j))

    kv_shape = jax.ShapeDtypeStruct((b, HEADS, t, HEAD_DIM), F32)
    kv_spec = pl.BlockSpec((1, 2, t, HEAD_DIM), lambda i, j: (i, j, 0, 0))
    return pl.pallas_call(
        functools.partial(_attn_prompt_kernel, seq=t),
        grid=(b, HEADS // 2),
        in_specs=[col_spec(0), col_spec(1), col_spec(2)],
        out_specs=[pl.BlockSpec((1, t, LANES), lambda i, j: (i, 0, j)), kv_spec, kv_spec],
        out_shape=[jax.ShapeDtypeStruct((b, t, D_MODEL), F32), kv_shape, kv_shape],
        scratch_shapes=[pltpu.VMEM((2, t, LANES), BF16),
                        pltpu.VMEM((padded, LANES), BF16),
                        pltpu.VMEM((2, padded, LANES), BF16)],
        compiler_params=_params(2),
        name=name,
    )(p3, p3, p3)


def _attn_sample_kernel(q_ref, k_ref, v_ref, ck_ref, cv_ref, o_ref, nk_ref, nv_ref, *, past, new):
    n_full = past // KEY_BLOCK
    tail = past - n_full * KEY_BLOCK
    m2 = _suffix_sum_matrix()
    row = lax.broadcasted_iota(jnp.int32, (new, KEY_BLOCK), 0)
    col = lax.broadcasted_iota(jnp.int32, (new, KEY_BLOCK), 1)
    mask = col < row + tail
    zero_rows = jnp.zeros((KEY_BLOCK - tail - new, HEAD_DIM), F32)
    zero = jnp.zeros((new, KEY_BLOCK), F32)

    q2 = q_ref[0] * Q_SCALE
    k2 = k_ref[0]
    v2 = v_ref[0]
    outs = []
    for h in range(2):
        lanes = slice(h * HEAD_DIM, (h + 1) * HEAD_DIM)
        q = q2[:, lanes].astype(BF16)
        kn = k2[:, lanes]
        vn = v2[:, lanes]
        nk_ref[0, h] = kn
        nv_ref[0, h] = vn
        kd = jnp.concatenate([ck_ref[0, 0, h, n_full * KEY_BLOCK:past, :], kn, zero_rows], axis=0)
        vd = jnp.concatenate([cv_ref[0, 0, h, n_full * KEY_BLOCK:past, :], vn, zero_rows], axis=0)
        later, acc = _sb_block(q, kd.astype(BF16), vd.astype(BF16), m2, zero,
                               jnp.zeros((new, HEAD_DIM), F32), mask)

        def body(i, carry, h=h, q=q):
            later, acc = carry
            kb0 = pl.multiple_of((n_full - 1 - i) * KEY_BLOCK, KEY_BLOCK)
            kb = ck_ref[0, 0, h, pl.ds(kb0, KEY_BLOCK), :].astype(BF16)
            vb = cv_ref[0, 0, h, pl.ds(kb0, KEY_BLOCK), :].astype(BF16)
            return _sb_block(q, kb, vb, m2, later, acc, None)

        _, acc = lax.fori_loop(0, n_full, body, (later, acc))
        outs.append(acc)
    o_ref[0] = jnp.concatenate(outs, axis=1)


def _attn_sample(p3, cache_k, cache_v, layer, name):
    b, new, _ = p3.shape
    past = cache_k.shape[3]
    assert past % KEY_BLOCK + new <= KEY_BLOCK
    col0 = 4 * A_WIDTH // LANES
    per = D_MODEL // LANES

    def col_spec(which):
        return pl.BlockSpec((1, new, LANES), lambda i, j: (i, 0, col0 + which * per + j))

    cache_spec = pl.BlockSpec((1, 1, 2, past, HEAD_DIM), lambda i, j: (layer, i, j, 0, 0))
    kv_shape = jax.ShapeDtypeStruct((b, HEADS, new, HEAD_DIM), F32)
    kv_spec = pl.BlockSpec((1, 2, new, HEAD_DIM), lambda i, j: (i, j, 0, 0))
    return pl.pallas_call(
        functools.partial(_attn_sample_kernel, past=past, new=new),
        grid=(b, HEADS // 2),
        in_specs=[col_spec(0), col_spec(1), col_spec(2), cache_spec, cache_spec],
        out_specs=[pl.BlockSpec((1, new, LANES), lambda i, j: (i, 0, j)), kv_spec, kv_spec],
        out_shape=[jax.ShapeDtypeStruct((b, new, D_MODEL), F32), kv_shape, kv_shape],
        compiler_params=_params(2),
        name=name,
    )(p3, p3, p3, cache_k, cache_v)


def _sigmoid(x):
    return 1.0 / (1.0 + jnp.exp(-x))


def _silu(x):
    return x * _sigmoid(x)


def _layer_norm(x, g, b):
    mu = jnp.mean(x, axis=-1, keepdims=True)
    d = x - mu
    var = jnp.mean(d * d, axis=-1, keepdims=True)
    return d * lax.rsqrt(var + LN_EPS) * g + b


def _even_mix_kernel(x_ref, h_ref, gb_ref, gc_ref, za_ref, zb_ref, o_ref, buf_ref, cw_ref, wo_ref,
                     g_ref, b_ref, y_ref, nbuf_ref, ext, *, tm):
    hist = A_CONV_W - 1
    pad = 8

    @pl.when(pl.program_id(1) == 0)
    def _():
        ext[pad - hist:pad, :] = buf_ref[0]

    @pl.when(pl.program_id(1) > 0)
    def _():
        ext[0:pad, :] = ext[tm:tm + pad, :]

    u = gc_ref[0] * h_ref[0]
    ext[pad:pad + tm, :] = u
    cw = cw_ref[...]
    conv = cw[2:3, :] * u
    for tap in range(hist):
        conv = conv + cw[tap:tap + 1, :] * ext[pad - hist + tap:pad - hist + tap + tm, :]
    nbuf_ref[0] = ext[tm + pad - hist:tm + pad, :]
    y_a = gb_ref[0] * conv * _silu(za_ref[0])
    y_b = o_ref[0] * _silu(zb_ref[0])
    out = jnp.dot(y_a.astype(BF16), wo_ref[0:A_WIDTH, :], preferred_element_type=F32)
    out = out + jnp.dot(y_b.astype(BF16), wo_ref[A_WIDTH:, :], preferred_element_type=F32)
    y_ref[0] = _layer_norm(ALPHA * x_ref[0] + out, g_ref[...], b_ref[...])


def _even_mix(x, p3, o, conv_buf, conv_w, w_out_bf16, ln_g, ln_b, tm, name):
    b, t, d = x.shape
    hist = A_CONV_W - 1

    def row_spec(col):
        return pl.BlockSpec((1, tm, d), lambda i, j: (i, j, col))

    def full_spec(shape):
        return pl.BlockSpec(shape, lambda i, j: (0,) * len(shape))

    buf_spec = pl.BlockSpec((1, hist, A_WIDTH), lambda i, j: (i, 0, 0))
    return pl.pallas_call(
        functools.partial(_even_mix_kernel, tm=tm),
        grid=(b, t // tm),
        in_specs=[row_spec(0), row_spec(0), row_spec(1), row_spec(2), row_spec(3), row_spec(7),
                  row_spec(0), buf_spec, full_spec((A_CONV_W, A_WIDTH)),
                  full_spec((2 * D_MODEL, D_MODEL)), full_spec((1, d)), full_spec((1, d))],
        out_specs=[row_spec(0), buf_spec],
        out_shape=[jax.ShapeDtypeStruct((b, t, d), F32),
                   jax.ShapeDtypeStruct((b, hist, A_WIDTH), F32)],
        scratch_shapes=[pltpu.VMEM((tm + 8, A_WIDTH), F32)],
        compiler_params=_params(2),
        name=name,
    )(x, p3, p3, p3, p3, p3, o, conv_buf, conv_w, w_out_bf16, ln_g[None], ln_b[None])


def _odd_mix_kernel(x_ref, a_ref, gate_ref, zc_ref, buf_ref, cw_ref, cb_ref, cg_ref, cbeta_ref, wo_ref,
                    g_ref, b_ref, y_ref, nbuf_ref, ext, conv_s, *, tm):
    hist = C_CONV_W - 1
    pad = 32

    @pl.when(pl.program_id(1) == 0)
    def _():
        ext[pad - hist:pad, :] = buf_ref[0]

    @pl.when(pl.program_id(1) > 0)
    def _():
        ext[0:pad, :] = ext[tm:tm + pad, :]

    ext[pad:pad + tm, :] = a_ref[0] * _sigmoid(gate_ref[0])
    nbuf_ref[0] = ext[tm + pad - hist:tm + pad, :]

    def lane_chunk(c, carry):
        lanes = pl.ds(pl.multiple_of(c * LANES, LANES), LANES)
        acc = jnp.zeros((tm, LANES), F32) + cb_ref[:, lanes]
        for tap in range(C_CONV_W):
            acc = acc + cw_ref[tap:tap + 1, lanes] * ext[pad - hist + tap:pad - hist + tap + tm, lanes]
        conv_s[:, lanes] = acc
        return carry

    lax.fori_loop(0, C_WIDTH // LANES, lane_chunk, 0)
    y = _silu(_layer_norm(conv_s[...], cg_ref[...], cbeta_ref[...])) * _silu(zc_ref[0])
    out = jnp.dot(y.astype(BF16), wo_ref[...], preferred_element_type=F32)
    y_ref[0] = _layer_norm(ALPHA * x_ref[0] + out, g_ref[...], b_ref[...])


def _odd_mix(x, p3, conv_buf, conv_w, conv_b, ln_c_g, ln_c_b, w_out_bf16, ln_g, ln_b, tm, name):
    b, t, d = x.shape
    hist = C_CONV_W - 1

    def row_spec(width, col):
        return pl.BlockSpec((1, tm, width), lambda i, j: (i, j, col))

    def full_spec(shape):
        return pl.BlockSpec(shape, lambda i, j: (0,) * len(shape))

    buf_spec = pl.BlockSpec((1, hist, C_WIDTH), lambda i, j: (i, 0, 0))
    return pl.pallas_call(
        functools.partial(_odd_mix_kernel, tm=tm),
        grid=(b, t // tm),
        in_specs=[row_spec(d, 0), row_spec(C_WIDTH, 0), row_spec(C_WIDTH, 1), row_spec(C_WIDTH, 2),
                  buf_spec, full_spec((C_CONV_W, C_WIDTH)), full_spec((1, C_WIDTH)),
                  full_spec((1, C_WIDTH)), full_spec((1, C_WIDTH)), full_spec((C_WIDTH, D_MODEL)),
                  full_spec((1, d)), full_spec((1, d))],
        out_specs=[row_spec(d, 0), buf_spec],
        out_shape=[jax.ShapeDtypeStruct((b, t, d), F32),
                   jax.ShapeDtypeStruct((b, hist, C_WIDTH), F32)],
        scratch_shapes=[pltpu.VMEM((tm + 32, C_WIDTH), F32), pltpu.VMEM((tm, C_WIDTH), F32)],
        compiler_params=_params(2),
        name=name,
    )(x, p3, p3, p3, conv_buf, conv_w, conv_b[None], ln_c_g[None], ln_c_b[None], w_out_bf16,
      ln_g[None], ln_b[None])


PROMPT_ROW_TILE = 344
PROJ_COL_TILE = 512


def _trunk(x, conv_a_bufs, conv_c_bufs, cache_k, cache_v, weights, tag):
    (w_in_even, conv_a_w, w_out_even, w_in_odd, conv_c_w, conv_c_b, ln_c_g, ln_c_b, w_out_odd,
     post_ln_g, post_ln_b) = weights
    b, t, d = x.shape
    prompt = cache_k is None
    proj_tm = t if prompt else b * t
    mix_tm = PROMPT_ROW_TILE if prompt else t
    new_k, new_v, new_a, new_c = [], [], [], []
    for layer in range(DEPTH):
        i = layer // 2
        nm = f"{tag}{layer}"
        if layer % 2 == 0:
            p3 = _proj(x.reshape(b * t, d), w_in_even[i], proj_tm, PROJ_COL_TILE,
                       f"proj_{nm}").reshape(b, t, EVEN_IN)
            if prompt:
                o, kh, vh = _attn_prompt(p3, f"attn_{nm}")
            else:
                o, kh, vh = _attn_sample(p3, cache_k, cache_v, i, f"attn_{nm}")
            x, buf = _even_mix(x, p3, o, conv_a_bufs[i], conv_a_w[i], w_out_even[i],
                               post_ln_g[layer], post_ln_b[layer], mix_tm, f"mix_{nm}")
            new_k.append(kh)
            new_v.append(vh)
            new_a.append(buf)
        else:
            p3 = _proj(x.reshape(b * t, d), w_in_odd[i], proj_tm, PROJ_COL_TILE,
                       f"proj_{nm}").reshape(b, t, ODD_IN)
            x, buf = _odd_mix(x, p3, conv_c_bufs[i], conv_c_w[i], conv_c_b[i], ln_c_g[i], ln_c_b[i],
                              w_out_odd[i], post_ln_g[layer], post_ln_b[layer], mix_tm, f"mix_{nm}")
            new_c.append(buf)
    return x, jnp.stack(new_k), jnp.stack(new_v), jnp.stack(new_a), jnp.stack(new_c)


def kernel(x_prompt, x_sample, cache_sb_k, cache_sb_v, state_conv_a, state_conv_c, meta_tokens,
           w_in_even, conv_a_w, w_out_even, w_in_odd, conv_c_w, conv_c_b, ln_c_g, ln_c_b,
           w_out_odd, post_ln_g, post_ln_b):
    weights = (w_in_even.astype(BF16), conv_a_w, w_out_even.astype(BF16), w_in_odd.astype(BF16),
               conv_c_w, conv_c_b, ln_c_g, ln_c_b, w_out_odd.astype(BF16), post_ln_g, post_ln_b)
    b_p = x_prompt.shape[0]
    n_even = w_in_even.shape[0]
    n_odd = w_in_odd.shape[0]
    meta = jnp.broadcast_to(meta_tokens.astype(x_prompt.dtype)[None], (b_p, N_META, D_MODEL))
    xp = jnp.concatenate([meta, x_prompt], axis=1)
    zeros_a = jnp.zeros((n_even, b_p, A_CONV_W - 1, A_WIDTH), x_prompt.dtype)
    zeros_c = jnp.zeros((n_odd, b_p, C_CONV_W - 1, C_WIDTH), x_prompt.dtype)
    h_p, k_p, v_p, a_p, c_p = _trunk(xp, zeros_a, zeros_c, None, None, weights, "p")
    y_sample, k_s, v_s, a_s, c_s = _trunk(x_sample, state_conv_a, state_conv_c, cache_sb_k,
                                          cache_sb_v, weights, "s")
    return (h_p[:, N_META:], y_sample, k_p, v_p, a_p, c_p, k_s, v_s, a_s, c_s)
```

```python
import functools
import math

import jax
import jax.numpy as jnp
from jax import lax
from jax.experimental import pallas as pl
from jax.experimental.pallas import tpu as pltpu

F32 = jnp.float32
BF16 = jnp.bfloat16

D_MODEL = 1024
DEPTH = 4
N_META = 16
HEADS = 16
HEAD_DIM = 64
A_WIDTH = D_MODEL
A_CONV_W = 3
C_WIDTH = 2 * D_MODEL
C_CONV_W = 31
EVEN_IN = 8 * D_MODEL
ODD_IN = 3 * C_WIDTH
ALPHA = (2 * DEPTH) ** 0.25
LN_EPS = 1e-5
Q_SCALE = 1.0 / math.sqrt(HEAD_DIM)

LANES = 128
KEY_BLOCK = 128
VMEM_LIMIT = 56 * 1024 * 1024


def _params(n_axes):
    return pltpu.CompilerParams(dimension_semantics=("arbitrary",) * n_axes,
                                vmem_limit_bytes=VMEM_LIMIT)


def _proj_kernel(x_ref, w_ref, o_ref, xb_ref):
    @pl.when(pl.program_id(1) == 0)
    def _():
        xb_ref[...] = x_ref[...].astype(BF16)

    o_ref[...] = jnp.dot(xb_ref[...], w_ref[...], preferred_element_type=F32)


def _proj(x2d, w_bf16, tm, tn, name):
    m, k = x2d.shape
    n = w_bf16.shape[1]
    return pl.pallas_call(
        _proj_kernel,
        grid=(m // tm, n // tn),
        in_specs=[pl.BlockSpec((tm, k), lambda i, j: (i, 0)),
                  pl.BlockSpec((k, tn), lambda i, j: (0, j))],
        out_specs=pl.BlockSpec((tm, tn), lambda i, j: (i, j)),
        out_shape=jax.ShapeDtypeStruct((m, n), F32),
        scratch_shapes=[pltpu.VMEM((tm, k), BF16)],
        compiler_params=_params(2),
        name=name,
    )(x2d, w_bf16)


def _suffix_sum_matrix():
    r = lax.broadcasted_iota(jnp.int32, (2 * KEY_BLOCK, 2 * KEY_BLOCK), 0) & (KEY_BLOCK - 1)
    c = lax.broadcasted_iota(jnp.int32, (2 * KEY_BLOCK, 2 * KEY_BLOCK), 1)
    return jnp.where((c >= KEY_BLOCK) | (r > c), 1.0, 0.0).astype(BF16)


def _sb_chunk(q, kt, vc, m2, later, acc, mask):
    z = jnp.dot(q, kt, preferred_element_type=F32)
    lse = jnp.log(1.0 + jnp.exp(-jnp.abs(z)))
    log_beta = jnp.minimum(z, 0.0) - lse
    log_1m = log_beta - z
    if mask is not None:
        log_1m = jnp.where(mask, log_1m, 0.0)

    def suffix(x):
        hi = x.astype(BF16)
        lo = (x - hi.astype(F32)).astype(BF16)
        return jnp.dot(jnp.concatenate([hi, lo], axis=1), m2, preferred_element_type=F32)

    c_right = suffix(log_1m[:, KEY_BLOCK:])
    c_left = suffix(log_1m[:, :KEY_BLOCK])
    later_left = later + c_right[:, KEY_BLOCK:]
    between = jnp.concatenate([c_left[:, :KEY_BLOCK] + later_left, c_right[:, :KEY_BLOCK] + later], axis=1)
    a = jnp.exp(log_beta + between)
    if mask is not None:
        a = jnp.where(mask, a, 0.0)
    acc = acc + jnp.dot(a.astype(BF16), vc, preferred_element_type=F32)
    return later_left + c_left[:, KEY_BLOCK:], acc


def _attn_prompt_kernel(q_ref, k_ref, v_ref, *rest, seq, first_layer):
    o_ref, nk_ref, nv_ref, qs, kts, vs = rest[-6:]
    chunk = 2 * KEY_BLOCK
    n_full = seq // chunk
    tail = seq - n_full * chunk
    padded = (n_full + 1) * chunk
    m2 = _suffix_sum_matrix()
    lane = lax.broadcasted_iota(jnp.int32, (seq, LANES), 1)
    first = lane < HEAD_DIM

    k = k_ref[0]
    v = v_ref[0]
    q = q_ref[0] * Q_SCALE
    nk_ref[0, 0, 0] = k[:, :HEAD_DIM]
    nk_ref[0, 0, 1] = k[:, HEAD_DIM:]
    nv_ref[0, 0, 0] = v[:, :HEAD_DIM]
    nv_ref[0, 0, 1] = v[:, HEAD_DIM:]
    if first_layer:
        for later_slot in range(1, nk_ref.shape[0]):
            nk_ref[later_slot] = jnp.zeros(nk_ref.shape[1:], F32)
            nv_ref[later_slot] = jnp.zeros(nv_ref.shape[1:], F32)
    qs[0] = jnp.where(first, q, 0.0).astype(BF16)
    qs[1] = jnp.where(first, 0.0, q).astype(BF16)
    vs[0, 0:seq, :] = jnp.where(first, v, 0.0).astype(BF16)
    vs[1, 0:seq, :] = jnp.where(first, 0.0, v).astype(BF16)
    zpad = jnp.zeros((padded - seq, LANES), BF16)
    vs[0, seq:padded, :] = zpad
    vs[1, seq:padded, :] = zpad
    for j in range(n_full):
        kts[:, j * chunk:(j + 1) * chunk] = k[j * chunk:(j + 1) * chunk, :].T.astype(BF16)
    k_tail = jnp.concatenate([k[n_full * chunk:, :], jnp.zeros((chunk - tail, LANES), F32)], axis=0)
    kts[:, n_full * chunk:] = k_tail.T.astype(BF16)

    def q_tile(base, rows, n_before):
        row = lax.broadcasted_iota(jnp.int32, (rows, chunk), 0)
        col = lax.broadcasted_iota(jnp.int32, (rows, chunk), 1)
        mask = col < row
        q0 = qs[0, pl.ds(base, rows), :]
        q1 = qs[1, pl.ds(base, rows), :]
        kd = kts[:, pl.ds(base, chunk)]
        zero = jnp.zeros((rows, KEY_BLOCK), F32)
        l0, acc = _sb_chunk(q0, kd, vs[0, pl.ds(base, chunk), :], m2, zero, zero, mask)
        l1, acc = _sb_chunk(q1, kd, vs[1, pl.ds(base, chunk), :], m2, zero, acc, mask)

        def body(i, carry):
            l0, l1, acc = carry
            kb0 = pl.multiple_of((n_before - 1 - i) * chunk, chunk)
            kt = kts[:, pl.ds(kb0, chunk)]
            l0, acc = _sb_chunk(q0, kt, vs[0, pl.ds(kb0, chunk), :], m2, l0, acc, None)
            l1, acc = _sb_chunk(q1, kt, vs[1, pl.ds(kb0, chunk), :], m2, l1, acc, None)
            return l0, l1, acc

        _, _, acc = lax.fori_loop(0, n_before, body, (l0, l1, acc))
        o_ref[0, pl.ds(base, rows), :] = acc

    def full_tile(m, carry):
        q_tile(pl.multiple_of(m * chunk, chunk), chunk, m)
        return carry

    lax.fori_loop(0, n_full, full_tile, 0)
    if tail:
        q_tile(n_full * chunk, tail, n_full)


def _attn_prompt(p3, slot, n_slots, stacked_kv, name):
    b, t, _ = p3.shape
    padded = (t // (2 * KEY_BLOCK) + 1) * 2 * KEY_BLOCK
    col0 = 4 * A_WIDTH // LANES
    per = D_MODEL // LANES
    first = stacked_kv is None
    assert first == (slot == 0)

    def col_spec(which):
        return pl.BlockSpec((1, t, LANES), lambda i, j: (i, 0, col0 + which * per + j))

    kv_shape = jax.ShapeDtypeStruct((n_slots, b, HEADS, t, HEAD_DIM), F32)
    if first:
        kv_spec = pl.BlockSpec((n_slots, 1, 2, t, HEAD_DIM), lambda i, j: (0, i, j, 0, 0))
        extra_specs, extra_args, aliases = [], (), {}
    else:
        kv_spec = pl.BlockSpec((1, 1, 2, t, HEAD_DIM), lambda i, j: (slot, i, j, 0, 0))
        extra_specs = [pl.BlockSpec(memory_space=pl.ANY)] * 2
        extra_args = tuple(stacked_kv)
        aliases = {3: 1, 4: 2}
    return pl.pallas_call(
        functools.partial(_attn_prompt_kernel, seq=t, first_layer=first),
        grid=(b, HEADS // 2),
        in_specs=[col_spec(0), col_spec(1), col_spec(2)] + extra_specs,
        out_specs=[pl.BlockSpec((1, t, LANES), lambda i, j: (i, 0, j)), kv_spec, kv_spec],
        out_shape=[jax.ShapeDtypeStruct((b, t, D_MODEL), F32), kv_shape, kv_shape],
        scratch_shapes=[pltpu.VMEM((2, t, LANES), BF16),
                        pltpu.VMEM((LANES, padded), BF16),
                        pltpu.VMEM((2, padded, LANES), BF16)],
        input_output_aliases=aliases,
        compiler_params=_params(2),
        name=name,
    )(p3, p3, p3, *extra_args)


def _attn_sample_kernel(q_ref, k_ref, v_ref, ck_ref, cv_ref, o_ref, nk_ref, nv_ref, *, past, new):
    n_blocks = (past + new + KEY_BLOCK - 1) // KEY_BLOCK
    total = n_blocks * KEY_BLOCK
    last = (n_blocks - 1) * KEY_BLOCK
    m2 = _suffix_sum_matrix()
    row = lax.broadcasted_iota(jnp.int32, (new, KEY_BLOCK), 0)
    col = lax.broadcasted_iota(jnp.int32, (new, KEY_BLOCK), 1)
    mask = col + last < row + past
    zero_rows = jnp.zeros((total - past - new, HEAD_DIM), BF16)

    q2 = q_ref[0] * Q_SCALE
    k2 = k_ref[0]
    v2 = v_ref[0]
    outs = []
    for h in range(2):
        lanes = slice(h * HEAD_DIM, (h + 1) * HEAD_DIM)
        q = q2[:, lanes].astype(BF16)
        kn = k2[:, lanes]
        vn = v2[:, lanes]
        nk_ref[0, h] = kn
        nv_ref[0, h] = vn
        k_all = jnp.concatenate([ck_ref[0, 0, h].astype(BF16), kn.astype(BF16), zero_rows], axis=0)
        v_all = jnp.concatenate([cv_ref[0, 0, h].astype(BF16), vn.astype(BF16), zero_rows], axis=0)
        z = lax.dot_general(q, k_all, (((1,), (1,)), ((), ())), preferred_element_type=F32)
        lse = jnp.log(1.0 + jnp.exp(-jnp.abs(z)))
        log_beta = jnp.minimum(z, 0.0) - lse
        log_1m = log_beta - z
        blocks = [log_1m[:, j * KEY_BLOCK:(j + 1) * KEY_BLOCK] for j in range(n_blocks)]
        blocks[-1] = jnp.where(mask, blocks[-1], 0.0)
        stacked = jnp.concatenate(blocks, axis=0)
        hi = stacked.astype(BF16)
        lo = (stacked - hi.astype(F32)).astype(BF16)
        c = jnp.dot(jnp.concatenate([hi, lo], axis=1), m2, preferred_element_type=F32)
        later = jnp.zeros((new, KEY_BLOCK), F32)
        between = [None] * n_blocks
        for j in reversed(range(n_blocks)):
            cj = c[j * new:(j + 1) * new]
            between[j] = cj[:, :KEY_BLOCK] + later
            later = later + cj[:, KEY_BLOCK:]
        a = jnp.exp(log_beta + jnp.concatenate(between, axis=1))
        a = jnp.concatenate([a[:, :last], jnp.where(mask, a[:, last:], 0.0)], axis=1)
        outs.append(jnp.dot(a.astype(BF16), v_all, preferred_element_type=F32))
    o_ref[0] = jnp.concatenate(outs, axis=1)


def _attn_sample(p3, cache_k, cache_v, layer, name):
    b, new, _ = p3.shape
    past = cache_k.shape[3]
    assert past % KEY_BLOCK + new <= KEY_BLOCK
    col0 = 4 * A_WIDTH // LANES
    per = D_MODEL // LANES

    def col_spec(which):
        return pl.BlockSpec((1, new, LANES), lambda i, j: (i, 0, col0 + which * per + j))

    cache_spec = pl.BlockSpec((1, 1, 2, past, HEAD_DIM), lambda i, j: (layer, i, j, 0, 0))
    kv_shape = jax.ShapeDtypeStruct((b, HEADS, new, HEAD_DIM), F32)
    kv_spec = pl.BlockSpec((1, 2, new, HEAD_DIM), lambda i, j: (i, j, 0, 0))
    return pl.pallas_call(
        functools.partial(_attn_sample_kernel, past=past, new=new),
        grid=(b, HEADS // 2),
        in_specs=[col_spec(0), col_spec(1), col_spec(2), cache_spec, cache_spec],
        out_specs=[pl.BlockSpec((1, new, LANES), lambda i, j: (i, 0, j)), kv_spec, kv_spec],
        out_shape=[jax.ShapeDtypeStruct((b, new, D_MODEL), F32), kv_shape, kv_shape],
        compiler_params=_params(2),
        name=name,
    )(p3, p3, p3, cache_k, cache_v)


def _sigmoid(x):
    return 1.0 / (1.0 + jnp.exp(-x))


def _silu(x):
    return x * _sigmoid(x)


def _layer_norm(x, g, b):
    mu = jnp.mean(x, axis=-1, keepdims=True)
    d = x - mu
    var = jnp.mean(d * d, axis=-1, keepdims=True)
    return d * lax.rsqrt(var + LN_EPS) * g + b


def _even_mix_kernel(x_ref, h_ref, gb_ref, gc_ref, za_ref, zb_ref, o_ref, buf_ref, cw_ref, wo_ref,
                     g_ref, b_ref, y_ref, nbuf_ref, ext, *, tm):
    hist = A_CONV_W - 1
    pad = 8

    @pl.when(pl.program_id(1) == 0)
    def _():
        ext[pad - hist:pad, :] = buf_ref[0]

    @pl.when(pl.program_id(1) > 0)
    def _():
        ext[0:pad, :] = ext[tm:tm + pad, :]

    u = gc_ref[0] * h_ref[0]
    ext[pad:pad + tm, :] = u
    cw = cw_ref[...]
    conv = cw[2:3, :] * u
    for tap in range(hist):
        conv = conv + cw[tap:tap + 1, :] * ext[pad - hist + tap:pad - hist + tap + tm, :]
    nbuf_ref[0] = ext[tm + pad - hist:tm + pad, :]
    y_a = gb_ref[0] * conv * _silu(za_ref[0])
    y_b = o_ref[0] * _silu(zb_ref[0])
    out = jnp.dot(y_a.astype(BF16), wo_ref[0:A_WIDTH, :], preferred_element_type=F32)
    out = out + jnp.dot(y_b.astype(BF16), wo_ref[A_WIDTH:, :], preferred_element_type=F32)
    y_ref[0] = _layer_norm(ALPHA * x_ref[0] + out, g_ref[...], b_ref[...])


def _even_mix(x, p3, o, conv_buf, conv_w, w_out_bf16, ln_g, ln_b, tm, name):
    b, t, d = x.shape
    hist = A_CONV_W - 1

    def row_spec(col):
        return pl.BlockSpec((1, tm, d), lambda i, j: (i, j, col))

    def full_spec(shape):
        return pl.BlockSpec(shape, lambda i, j: (0,) * len(shape))

    buf_spec = pl.BlockSpec((1, hist, A_WIDTH), lambda i, j: (i, 0, 0))
    return pl.pallas_call(
        functools.partial(_even_mix_kernel, tm=tm),
        grid=(b, t // tm),
        in_specs=[row_spec(0), row_spec(0), row_spec(1), row_spec(2), row_spec(3), row_spec(7),
                  row_spec(0), buf_spec, full_spec((A_CONV_W, A_WIDTH)),
                  full_spec((2 * D_MODEL, D_MODEL)), full_spec((1, d)), full_spec((1, d))],
        out_specs=[row_spec(0), buf_spec],
        out_shape=[jax.ShapeDtypeStruct((b, t, d), F32),
                   jax.ShapeDtypeStruct((b, hist, A_WIDTH), F32)],
        scratch_shapes=[pltpu.VMEM((tm + 8, A_WIDTH), F32)],
        compiler_params=_params(2),
        name=name,
    )(x, p3, p3, p3, p3, p3, o, conv_buf, conv_w, w_out_bf16, ln_g[None], ln_b[None])


def _odd_mix_kernel(x_ref, a_ref, gate_ref, zc_ref, buf_ref, cw_ref, cb_ref, cg_ref, cbeta_ref, wo_ref,
                    g_ref, b_ref, y_ref, nbuf_ref, ext, conv_s, shifted, *, tm):
    hist = C_CONV_W - 1
    pad = 32

    @pl.when(pl.program_id(1) == 0)
    def _():
        ext[pad - hist:pad, :] = buf_ref[0]

    @pl.when(pl.program_id(1) > 0)
    def _():
        ext[0:pad, :] = ext[tm:tm + pad, :]

    ext[pad:pad + tm, :] = a_ref[0] * _sigmoid(gate_ref[0])
    nbuf_ref[0] = ext[tm + pad - hist:tm + pad, :]

    sub = 8

    def lane_chunk(c, carry):
        lanes = pl.ds(pl.multiple_of(c * LANES, LANES), LANES)
        for phase in range(sub):
            span = tm + pad if phase == 0 else tm + pad - sub
            shifted[phase, 0:span, :] = ext[phase:phase + span, lanes]
        acc = jnp.zeros((tm, LANES), F32) + cb_ref[:, lanes]
        for tap in range(C_CONV_W):
            first_row = pad - hist + tap
            phase = first_row % sub
            acc = acc + cw_ref[tap:tap + 1, lanes] * shifted[phase, first_row - phase:first_row - phase + tm, :]
        conv_s[:, lanes] = acc
        return carry

    lax.fori_loop(0, C_WIDTH // LANES, lane_chunk, 0)
    y = _silu(_layer_norm(conv_s[...], cg_ref[...], cbeta_ref[...])) * _silu(zc_ref[0])
    out = jnp.dot(y.astype(BF16), wo_ref[...], preferred_element_type=F32)
    y_ref[0] = _layer_norm(ALPHA * x_ref[0] + out, g_ref[...], b_ref[...])


def _odd_mix(x, p3, conv_buf, conv_w, conv_b, ln_c_g, ln_c_b, w_out_bf16, ln_g, ln_b, tm, name):
    b, t, d = x.shape
    hist = C_CONV_W - 1

    def row_spec(width, col):
        return pl.BlockSpec((1, tm, width), lambda i, j: (i, j, col))

    def full_spec(shape):
        return pl.BlockSpec(shape, lambda i, j: (0,) * len(shape))

    buf_spec = pl.BlockSpec((1, hist, C_WIDTH), lambda i, j: (i, 0, 0))
    return pl.pallas_call(
        functools.partial(_odd_mix_kernel, tm=tm),
        grid=(b, t // tm),
        in_specs=[row_spec(d, 0), row_spec(C_WIDTH, 0), row_spec(C_WIDTH, 1), row_spec(C_WIDTH, 2),
                  buf_spec, full_spec((C_CONV_W, C_WIDTH)), full_spec((1, C_WIDTH)),
                  full_spec((1, C_WIDTH)), full_spec((1, C_WIDTH)), full_spec((C_WIDTH, D_MODEL)),
                  full_spec((1, d)), full_spec((1, d))],
        out_specs=[row_spec(d, 0), buf_spec],
        out_shape=[jax.ShapeDtypeStruct((b, t, d), F32),
                   jax.ShapeDtypeStruct((b, hist, C_WIDTH), F32)],
        scratch_shapes=[pltpu.VMEM((tm + 32, C_WIDTH), F32), pltpu.VMEM((tm, C_WIDTH), F32),
                        pltpu.VMEM((8, tm + 32, LANES), F32)],
        compiler_params=_params(2),
        name=name,
    )(x, p3, p3, p3, conv_buf, conv_w, conv_b[None], ln_c_g[None], ln_c_b[None], w_out_bf16,
      ln_g[None], ln_b[None])


PROMPT_ROW_TILE = 344
PROJ_COL_TILE = 512


def _trunk(x, conv_a_bufs, conv_c_bufs, cache_k, cache_v, weights, tag):
    (w_in_even, conv_a_w, w_out_even, w_in_odd, conv_c_w, conv_c_b, ln_c_g, ln_c_b, w_out_odd,
     post_ln_g, post_ln_b) = weights
    b, t, d = x.shape
    prompt = cache_k is None
    proj_tm = t if prompt else b * t
    mix_tm = PROMPT_ROW_TILE if prompt else t
    n_even = w_in_even.shape[0]
    new_k, new_v, new_a, new_c = [], [], [], []
    stacked_kv = None
    for layer in range(DEPTH):
        i = layer // 2
        nm = f"{tag}{layer}"
        if layer % 2 == 0:
            p3 = _proj(x.reshape(b * t, d), w_in_even[i], proj_tm, PROJ_COL_TILE,
                       f"proj_{nm}").reshape(b, t, EVEN_IN)
            if prompt:
                o, *stacked_kv = _attn_prompt(p3, i, n_even, stacked_kv, f"attn_{nm}")
            else:
                o, kh, vh = _attn_sample(p3, cache_k, cache_v, i, f"attn_{nm}")
                new_k.append(kh)
                new_v.append(vh)
            x, buf = _even_mix(x, p3, o, conv_a_bufs[i], conv_a_w[i], w_out_even[i],
                               post_ln_g[layer], post_ln_b[layer], mix_tm, f"mix_{nm}")
            new_a.append(buf)
        else:
            p3 = _proj(x.reshape(b * t, d), w_in_odd[i], proj_tm, PROJ_COL_TILE,
                       f"proj_{nm}").reshape(b, t, ODD_IN)
            x, buf = _odd_mix(x, p3, conv_c_bufs[i], conv_c_w[i], conv_c_b[i], ln_c_g[i], ln_c_b[i],
                              w_out_odd[i], post_ln_g[layer], post_ln_b[layer], mix_tm, f"mix_{nm}")
            new_c.append(buf)
    k_all, v_all = stacked_kv if prompt else (jnp.stack(new_k), jnp.stack(new_v))
    return x, k_all, v_all, jnp.stack(new_a), jnp.stack(new_c)


def kernel(x_prompt, x_sample, cache_sb_k, cache_sb_v, state_conv_a, state_conv_c, meta_tokens,
           w_in_even, conv_a_w, w_out_even, w_in_odd, conv_c_w, conv_c_b, ln_c_g, ln_c_b,
           w_out_odd, post_ln_g, post_ln_b):
    weights = (w_in_even.astype(BF16), conv_a_w, w_out_even.astype(BF16), w_in_odd.astype(BF16),
               conv_c_w, conv_c_b, ln_c_g, ln_c_b, w_out_odd.astype(BF16), post_ln_g, post_ln_b)
    b_p = x_prompt.shape[0]
    n_even = w_in_even.shape[0]
    n_odd = w_in_odd.shape[0]
    meta = jnp.broadcast_to(meta_tokens.astype(x_prompt.dtype)[None], (b_p, N_META, D_MODEL))
    xp = jnp.concatenate([meta, x_prompt], axis=1)
    zeros_a = jnp.zeros((n_even, b_p, A_CONV_W - 1, A_WIDTH), x_prompt.dtype)
    zeros_c = jnp.zeros((n_odd, b_p, C_CONV_W - 1, C_WIDTH), x_prompt.dtype)
    h_p, k_p, v_p, a_p, c_p = _trunk(xp, zeros_a, zeros_c, None, None, weights, "p")
    y_sample, k_s, v_s, a_s, c_s = _trunk(x_sample, state_conv_a, state_conv_c, cache_sb_k,
                                          cache_sb_v, weights, "s")
    return (h_p[:, N_META:], y_sample, k_p, v_p, a_p, c_p, k_s, v_s, a_s, c_s)
```

```python
import functools
import math

import jax
import jax.numpy as jnp
from jax import lax
from jax.experimental import pallas as pl
from jax.experimental.pallas import tpu as pltpu

F32 = jnp.float32
BF16 = jnp.bfloat16

D_MODEL = 1024
DEPTH = 4
N_META = 16
HEADS = 16
HEAD_DIM = 64
A_WIDTH = D_MODEL
A_CONV_W = 3
C_WIDTH = 2 * D_MODEL
C_CONV_W = 31
EVEN_IN = 8 * D_MODEL
ODD_IN = 3 * C_WIDTH
ALPHA = (2 * DEPTH) ** 0.25
LN_EPS = 1e-5
Q_SCALE = 1.0 / math.sqrt(HEAD_DIM)

LANES = 128
KEY_BLOCK = 128
VMEM_LIMIT = 56 * 1024 * 1024


def _params(n_axes):
    return pltpu.CompilerParams(dimension_semantics=("arbitrary",) * n_axes,
                                vmem_limit_bytes=VMEM_LIMIT)


def _proj_kernel(x_ref, w_ref, o_ref, xb_ref):
    @pl.when(pl.program_id(1) == 0)
    def _():
        xb_ref[...] = x_ref[...].astype(BF16)

    o_ref[...] = jnp.dot(xb_ref[...], w_ref[...], preferred_element_type=F32)


def _proj(x2d, w_bf16, tm, tn, name):
    m, k = x2d.shape
    n = w_bf16.shape[1]
    return pl.pallas_call(
        _proj_kernel,
        grid=(m // tm, n // tn),
        in_specs=[pl.BlockSpec((tm, k), lambda i, j: (i, 0)),
                  pl.BlockSpec((k, tn), lambda i, j: (0, j))],
        out_specs=pl.BlockSpec((tm, tn), lambda i, j: (i, j)),
        out_shape=jax.ShapeDtypeStruct((m, n), F32),
        scratch_shapes=[pltpu.VMEM((tm, k), BF16)],
        compiler_params=_params(2),
        name=name,
    )(x2d, w_bf16)


def _suffix_sum_matrix():
    r = lax.broadcasted_iota(jnp.int32, (2 * KEY_BLOCK, 2 * KEY_BLOCK), 0) & (KEY_BLOCK - 1)
    c = lax.broadcasted_iota(jnp.int32, (2 * KEY_BLOCK, 2 * KEY_BLOCK), 1)
    return jnp.where((c >= KEY_BLOCK) | (r > c), 1.0, 0.0).astype(BF16)


CHUNK = 2 * KEY_BLOCK
MASKED_LOGIT = -1e30


def _chunk_suffix_matrix():
    r = lax.broadcasted_iota(jnp.int32, (CHUNK, CHUNK + KEY_BLOCK), 0)
    c = lax.broadcasted_iota(jnp.int32, (CHUNK, CHUNK + KEY_BLOCK), 1)
    return jnp.where((c >= CHUNK) | (r > c), 1.0, 0.0).astype(BF16)


def _sb_scores(q, kt, msuf, mask):
    z = jnp.dot(q, kt, preferred_element_type=F32)
    lse = jnp.log(1.0 + jnp.exp(-jnp.abs(z)))
    log_beta = jnp.minimum(z, 0.0) - lse
    log_1m = log_beta - z
    if mask is not None:
        log_1m = jnp.where(mask, log_1m, 0.0)
    c = jnp.dot(log_1m.astype(BF16), msuf, preferred_element_type=F32)
    s = log_beta + c[:, :CHUNK]
    if mask is not None:
        s = jnp.where(mask, s, MASKED_LOGIT)
    return s, c[:, CHUNK:]


def _sb_apply(s, tot, vv, later, acc, rows):
    a = jnp.exp(s + jnp.concatenate([later, later], axis=1)).astype(BF16)
    a2 = jnp.concatenate([a[:rows], a[rows:]], axis=1)
    acc = acc + jnp.dot(a2, vv, preferred_element_type=F32)
    return later + tot, acc


def _attn_prompt_kernel(q_ref, k_ref, v_ref, *rest, seq, first_layer):
    o_ref, nk_ref, nv_ref, qs, kts, vs = rest[-6:]
    chunk = CHUNK
    n_full = seq // chunk
    tail = seq - n_full * chunk
    padded = (n_full + 1) * chunk
    msuf = _chunk_suffix_matrix()
    lane = lax.broadcasted_iota(jnp.int32, (seq, LANES), 1)
    first = lane < HEAD_DIM

    k = k_ref[0]
    v = v_ref[0]
    q = q_ref[0] * Q_SCALE
    if first_layer:
        for later_slot in range(1, nk_ref.shape[0]):
            nk_ref[later_slot] = jnp.zeros(nk_ref.shape[1:], F32)
            nv_ref[later_slot] = jnp.zeros(nv_ref.shape[1:], F32)
    qs[0] = jnp.where(first, q, 0.0).astype(BF16)
    qs[1] = jnp.where(first, 0.0, q).astype(BF16)
    vs[0, 0:seq, :] = jnp.where(first, v, 0.0).astype(BF16)
    vs[1, 0:seq, :] = jnp.where(first, 0.0, v).astype(BF16)
    zpad = jnp.zeros((padded - seq, LANES), BF16)
    vs[0, seq:padded, :] = zpad
    vs[1, seq:padded, :] = zpad
    zrows = jnp.zeros((chunk - tail, LANES), F32)
    for j in range(n_full + 1):
        width = chunk if j < n_full else tail
        cols = slice(j * chunk, j * chunk + width)
        if j < n_full:
            kt = k[cols, :].T
            vt = v[cols, :].T
        else:
            kt = jnp.concatenate([k[cols, :], zrows], axis=0).T
            vt = jnp.concatenate([v[cols, :], zrows], axis=0).T
        kts[:, j * chunk:(j + 1) * chunk] = kt.astype(BF16)
        for h in range(2):
            rows_h = slice(h * HEAD_DIM, (h + 1) * HEAD_DIM)
            nk_ref[0, 0, h, :, cols] = kt[rows_h, :width]
            nv_ref[0, 0, h, :, cols] = vt[rows_h, :width]

    def q_tile(base, rows, n_before):
        row = lax.broadcasted_iota(jnp.int32, (2 * rows, chunk), 0)
        col = lax.broadcasted_iota(jnp.int32, (2 * rows, chunk), 1)
        mask = col < jnp.where(row < rows, row, row - rows)
        qq = jnp.concatenate([qs[0, pl.ds(base, rows), :], qs[1, pl.ds(base, rows), :]], axis=0)
        zero = jnp.zeros((2 * rows, KEY_BLOCK), F32)

        def step(kb0, later, acc, mask):
            s, tot = _sb_scores(qq, kts[:, pl.ds(kb0, chunk)], msuf, mask)
            vv = jnp.concatenate([vs[0, pl.ds(kb0, chunk), :], vs[1, pl.ds(kb0, chunk), :]], axis=0)
            return _sb_apply(s, tot, vv, later, acc, rows)

        later, acc = step(base, zero, jnp.zeros((rows, LANES), F32), mask)

        def body(i, carry):
            kb0 = pl.multiple_of((n_before - 1 - i) * chunk, chunk)
            return step(kb0, carry[0], carry[1], None)

        _, acc = lax.fori_loop(0, n_before, body, (later, acc))
        o_ref[0, pl.ds(base, rows), :] = acc

    def full_tile(m, carry):
        q_tile(pl.multiple_of(m * chunk, chunk), chunk, m)
        return carry

    lax.fori_loop(0, n_full, full_tile, 0)
    if tail:
        q_tile(n_full * chunk, tail, n_full)


def _attn_prompt(p3, slot, n_slots, stacked_kv, name):
    b, t, _ = p3.shape
    padded = (t // (2 * KEY_BLOCK) + 1) * 2 * KEY_BLOCK
    col0 = 4 * A_WIDTH // LANES
    per = D_MODEL // LANES
    first = stacked_kv is None
    assert first == (slot == 0)

    def col_spec(which):
        return pl.BlockSpec((1, t, LANES), lambda i, j: (i, 0, col0 + which * per + j))

    kv_shape = jax.ShapeDtypeStruct((n_slots, b, HEADS, HEAD_DIM, t), F32)
    if first:
        kv_spec = pl.BlockSpec((n_slots, 1, 2, HEAD_DIM, t), lambda i, j: (0, i, j, 0, 0))
        extra_specs, extra_args, aliases = [], (), {}
    else:
        kv_spec = pl.BlockSpec((1, 1, 2, HEAD_DIM, t), lambda i, j: (slot, i, j, 0, 0))
        extra_specs = [pl.BlockSpec(memory_space=pl.ANY)] * 2
        extra_args = tuple(stacked_kv)
        aliases = {3: 1, 4: 2}
    return pl.pallas_call(
        functools.partial(_attn_prompt_kernel, seq=t, first_layer=first),
        grid=(b, HEADS // 2),
        in_specs=[col_spec(0), col_spec(1), col_spec(2)] + extra_specs,
        out_specs=[pl.BlockSpec((1, t, LANES), lambda i, j: (i, 0, j)), kv_spec, kv_spec],
        out_shape=[jax.ShapeDtypeStruct((b, t, D_MODEL), F32), kv_shape, kv_shape],
        scratch_shapes=[pltpu.VMEM((2, t, LANES), BF16),
                        pltpu.VMEM((LANES, padded), BF16),
                        pltpu.VMEM((2, padded, LANES), BF16)],
        input_output_aliases=aliases,
        compiler_params=_params(2),
        name=name,
    )(p3, p3, p3, *extra_args)


def _attn_sample_kernel(q_ref, k_ref, v_ref, ckt_ref, cvt_ref, ckl_ref, cvl_ref, o_ref, nk_ref, nv_ref,
                        *, past, new):
    n_blocks = (past + new + KEY_BLOCK - 1) // KEY_BLOCK
    total = n_blocks * KEY_BLOCK
    last = (n_blocks - 1) * KEY_BLOCK
    m2 = _suffix_sum_matrix()
    row = lax.broadcasted_iota(jnp.int32, (new, KEY_BLOCK), 0)
    col = lax.broadcasted_iota(jnp.int32, (new, KEY_BLOCK), 1)
    mask = col + last < row + past
    zero_rows = jnp.zeros((total - past - new, HEAD_DIM), F32)
    nt_dims = (((1,), (1,)), ((), ()))

    q2 = q_ref[0] * Q_SCALE
    k2 = k_ref[0]
    v2 = v_ref[0]
    outs = []
    for h in range(2):
        lanes = slice(h * HEAD_DIM, (h + 1) * HEAD_DIM)
        q = q2[:, lanes].astype(BF16)
        kn = k2[:, lanes]
        vn = v2[:, lanes]
        nk_ref[0, h] = kn
        nv_ref[0, h] = vn
        k_last = jnp.concatenate([ckl_ref[0, 0, h], kn, zero_rows], axis=0).astype(BF16)
        v_last = jnp.concatenate([cvl_ref[0, 0, h], vn, zero_rows], axis=0).astype(BF16)
        z = jnp.concatenate(
            [jnp.dot(q, ckt_ref[0, 0, h].astype(BF16), preferred_element_type=F32),
             lax.dot_general(q, k_last, nt_dims, preferred_element_type=F32)], axis=1)
        lse = jnp.log(1.0 + jnp.exp(-jnp.abs(z)))
        log_beta = jnp.minimum(z, 0.0) - lse
        log_1m = log_beta - z
        blocks = [log_1m[:, j * KEY_BLOCK:(j + 1) * KEY_BLOCK] for j in range(n_blocks)]
        blocks[-1] = jnp.where(mask, blocks[-1], 0.0)
        stacked = jnp.concatenate(blocks, axis=0)
        hi = stacked.astype(BF16)
        lo = (stacked - hi.astype(F32)).astype(BF16)
        c = jnp.dot(jnp.concatenate([hi, lo], axis=1), m2, preferred_element_type=F32)
        later = jnp.zeros((new, KEY_BLOCK), F32)
        between = [None] * n_blocks
        for j in reversed(range(n_blocks)):
            cj = c[j * new:(j + 1) * new]
            between[j] = cj[:, :KEY_BLOCK] + later
            later = later + cj[:, KEY_BLOCK:]
        a = jnp.exp(log_beta + jnp.concatenate(between, axis=1))
        a_last = jnp.where(mask, a[:, last:], 0.0).astype(BF16)
        out = lax.dot_general(a[:, :last].astype(BF16), cvt_ref[0, 0, h].astype(BF16), nt_dims,
                              preferred_element_type=F32)
        outs.append(out + jnp.dot(a_last, v_last, preferred_element_type=F32))
    o_ref[0] = jnp.concatenate(outs, axis=1)


def _attn_sample(p3, cache_t, cache_last, layer, name):
    b, new, _ = p3.shape
    past = cache_t[0].shape[4]
    last = past // KEY_BLOCK * KEY_BLOCK
    assert 0 < past - last and past - last + new <= KEY_BLOCK
    col0 = 4 * A_WIDTH // LANES
    per = D_MODEL // LANES

    def col_spec(which):
        return pl.BlockSpec((1, new, LANES), lambda i, j: (i, 0, col0 + which * per + j))

    cache_spec = pl.BlockSpec((1, 1, 2, HEAD_DIM, last), lambda i, j: (layer, i, j, 0, 0))
    last_spec = pl.BlockSpec((1, 1, 2, past - last, HEAD_DIM), lambda i, j: (layer, i, j, 0, 0))
    kv_shape = jax.ShapeDtypeStruct((b, HEADS, new, HEAD_DIM), F32)
    kv_spec = pl.BlockSpec((1, 2, new, HEAD_DIM), lambda i, j: (i, j, 0, 0))
    return pl.pallas_call(
        functools.partial(_attn_sample_kernel, past=past, new=new),
        grid=(b, HEADS // 2),
        in_specs=[col_spec(0), col_spec(1), col_spec(2), cache_spec, cache_spec, last_spec, last_spec],
        out_specs=[pl.BlockSpec((1, new, LANES), lambda i, j: (i, 0, j)), kv_spec, kv_spec],
        out_shape=[jax.ShapeDtypeStruct((b, new, D_MODEL), F32), kv_shape, kv_shape],
        compiler_params=_params(2),
        name=name,
    )(p3, p3, p3, *cache_t, *cache_last)


def _sigmoid(x):
    return 1.0 / (1.0 + jnp.exp(-x))


def _silu(x):
    return x * _sigmoid(x)


def _layer_norm(x, g, b):
    mu = jnp.mean(x, axis=-1, keepdims=True)
    d = x - mu
    var = jnp.mean(d * d, axis=-1, keepdims=True)
    return d * lax.rsqrt(var + LN_EPS) * g + b


def _even_mix_kernel(x_ref, h_ref, gb_ref, gc_ref, za_ref, zb_ref, o_ref, buf_ref, cw_ref, wo_ref,
                     g_ref, b_ref, y_ref, nbuf_ref, ext, *, tm):
    hist = A_CONV_W - 1
    pad = 8

    @pl.when(pl.program_id(1) == 0)
    def _():
        ext[pad - hist:pad, :] = buf_ref[0]

    @pl.when(pl.program_id(1) > 0)
    def _():
        ext[0:pad, :] = ext[tm:tm + pad, :]

    u = gc_ref[0] * h_ref[0]
    ext[pad:pad + tm, :] = u
    cw = cw_ref[...]
    conv = cw[2:3, :] * u
    for tap in range(hist):
        conv = conv + cw[tap:tap + 1, :] * ext[pad - hist + tap:pad - hist + tap + tm, :]
    nbuf_ref[0] = ext[tm + pad - hist:tm + pad, :]
    y_a = gb_ref[0] * conv * _silu(za_ref[0])
    y_b = o_ref[0] * _silu(zb_ref[0])
    out = jnp.dot(y_a.astype(BF16), wo_ref[0:A_WIDTH, :], preferred_element_type=F32)
    out = out + jnp.dot(y_b.astype(BF16), wo_ref[A_WIDTH:, :], preferred_element_type=F32)
    y_ref[0] = _layer_norm(ALPHA * x_ref[0] + out, g_ref[...], b_ref[...])


def _even_mix(x, p3, o, conv_buf, conv_w, w_out_bf16, ln_g, ln_b, tm, name):
    b, t, d = x.shape
    hist = A_CONV_W - 1

    def row_spec(col):
        return pl.BlockSpec((1, tm, d), lambda i, j: (i, j, col))

    def full_spec(shape):
        return pl.BlockSpec(shape, lambda i, j: (0,) * len(shape))

    buf_spec = pl.BlockSpec((1, hist, A_WIDTH), lambda i, j: (i, 0, 0))
    return pl.pallas_call(
        functools.partial(_even_mix_kernel, tm=tm),
        grid=(b, t // tm),
        in_specs=[row_spec(0), row_spec(0), row_spec(1), row_spec(2), row_spec(3), row_spec(7),
                  row_spec(0), buf_spec, full_spec((A_CONV_W, A_WIDTH)),
                  full_spec((2 * D_MODEL, D_MODEL)), full_spec((1, d)), full_spec((1, d))],
        out_specs=[row_spec(0), buf_spec],
        out_shape=[jax.ShapeDtypeStruct((b, t, d), F32),
                   jax.ShapeDtypeStruct((b, hist, A_WIDTH), F32)],
        scratch_shapes=[pltpu.VMEM((tm + 8, A_WIDTH), F32)],
        compiler_params=_params(2),
        name=name,
    )(x, p3, p3, p3, p3, p3, o, conv_buf, conv_w, w_out_bf16, ln_g[None], ln_b[None])


def _odd_mix_kernel(x_ref, a_ref, gate_ref, zc_ref, buf_ref, cw_ref, cb_ref, cg_ref, cbeta_ref, wo_ref,
                    g_ref, b_ref, y_ref, nbuf_ref, ext, conv_s, shifted, *, tm):
    hist = C_CONV_W - 1
    pad = 32

    @pl.when(pl.program_id(1) == 0)
    def _():
        ext[pad - hist:pad, :] = buf_ref[0]

    @pl.when(pl.program_id(1) > 0)
    def _():
        ext[0:pad, :] = ext[tm:tm + pad, :]

    ext[pad:pad + tm, :] = a_ref[0] * _sigmoid(gate_ref[0])
    nbuf_ref[0] = ext[tm + pad - hist:tm + pad, :]

    sub = 8

    def lane_chunk(c, carry):
        lanes = pl.ds(pl.multiple_of(c * LANES, LANES), LANES)
        for phase in range(sub):
            span = tm + pad if phase == 0 else tm + pad - sub
            shifted[phase, 0:span, :] = ext[phase:phase + span, lanes]
        acc = jnp.zeros((tm, LANES), F32) + cb_ref[:, lanes]
        for tap in range(C_CONV_W):
            first_row = pad - hist + tap
            phase = first_row % sub
            acc = acc + cw_ref[tap:tap + 1, lanes] * shifted[phase, first_row - phase:first_row - phase + tm, :]
        conv_s[:, lanes] = acc
        return carry

    lax.fori_loop(0, C_WIDTH // LANES, lane_chunk, 0)
    y = _silu(_layer_norm(conv_s[...], cg_ref[...], cbeta_ref[...])) * _silu(zc_ref[0])
    out = jnp.dot(y.astype(BF16), wo_ref[...], preferred_element_type=F32)
    y_ref[0] = _layer_norm(ALPHA * x_ref[0] + out, g_ref[...], b_ref[...])


def _odd_mix(x, p3, conv_buf, conv_w, conv_b, ln_c_g, ln_c_b, w_out_bf16, ln_g, ln_b, tm, name):
    b, t, d = x.shape
    hist = C_CONV_W - 1

    def row_spec(width, col):
        return pl.BlockSpec((1, tm, width), lambda i, j: (i, j, col))

    def full_spec(shape):
        return pl.BlockSpec(shape, lambda i, j: (0,) * len(shape))

    buf_spec = pl.BlockSpec((1, hist, C_WIDTH), lambda i, j: (i, 0, 0))
    return pl.pallas_call(
        functools.partial(_odd_mix_kernel, tm=tm),
        grid=(b, t // tm),
        in_specs=[row_spec(d, 0), row_spec(C_WIDTH, 0), row_spec(C_WIDTH, 1), row_spec(C_WIDTH, 2),
                  buf_spec, full_spec((C_CONV_W, C_WIDTH)), full_spec((1, C_WIDTH)),
                  full_spec((1, C_WIDTH)), full_spec((1, C_WIDTH)), full_spec((C_WIDTH, D_MODEL)),
                  full_spec((1, d)), full_spec((1, d))],
        out_specs=[row_spec(d, 0), buf_spec],
        out_shape=[jax.ShapeDtypeStruct((b, t, d), F32),
                   jax.ShapeDtypeStruct((b, hist, C_WIDTH), F32)],
        scratch_shapes=[pltpu.VMEM((tm + 32, C_WIDTH), F32), pltpu.VMEM((tm, C_WIDTH), F32),
                        pltpu.VMEM((8, tm + 32, LANES), F32)],
        compiler_params=_params(2),
        name=name,
    )(x, p3, p3, p3, conv_buf, conv_w, conv_b[None], ln_c_g[None], ln_c_b[None], w_out_bf16,
      ln_g[None], ln_b[None])


PROMPT_ROW_TILE = 344
PROJ_COL_TILE = 512


def _trunk(x, conv_a_bufs, conv_c_bufs, cache_k, cache_v, weights, tag):
    (w_in_even, conv_a_w, w_out_even, w_in_odd, conv_c_w, conv_c_b, ln_c_g, ln_c_b, w_out_odd,
     post_ln_g, post_ln_b) = weights
    b, t, d = x.shape
    prompt = cache_k is None
    if not prompt:
        whole = cache_k.shape[3] // KEY_BLOCK * KEY_BLOCK
        cache_t = (jnp.swapaxes(cache_k, 3, 4), jnp.swapaxes(cache_v, 3, 4))
        cache_last = (cache_k[:, :, :, whole:, :], cache_v[:, :, :, whole:, :])
    proj_tm = t if prompt else b * t
    mix_tm = PROMPT_ROW_TILE if prompt else t
    n_even = w_in_even.shape[0]
    new_k, new_v, new_a, new_c = [], [], [], []
    stacked_kv = None
    for layer in range(DEPTH):
        i = layer // 2
        nm = f"{tag}{layer}"
        if layer % 2 == 0:
            p3 = _proj(x.reshape(b * t, d), w_in_even[i], proj_tm, PROJ_COL_TILE,
                       f"proj_{nm}").reshape(b, t, EVEN_IN)
            if prompt:
                o, *stacked_kv = _attn_prompt(p3, i, n_even, stacked_kv, f"attn_{nm}")
            else:
                o, kh, vh = _attn_sample(p3, cache_t, cache_last, i, f"attn_{nm}")
                new_k.append(kh)
                new_v.append(vh)
            x, buf = _even_mix(x, p3, o, conv_a_bufs[i], conv_a_w[i], w_out_even[i],
                               post_ln_g[layer], post_ln_b[layer], mix_tm, f"mix_{nm}")
            new_a.append(buf)
        else:
            p3 = _proj(x.reshape(b * t, d), w_in_odd[i], proj_tm, PROJ_COL_TILE,
                       f"proj_{nm}").reshape(b, t, ODD_IN)
            x, buf = _odd_mix(x, p3, conv_c_bufs[i], conv_c_w[i], conv_c_b[i], ln_c_g[i], ln_c_b[i],
                              w_out_odd[i], post_ln_g[layer], post_ln_b[layer], mix_tm, f"mix_{nm}")
            new_c.append(buf)
    if prompt:
        k_all, v_all = (jnp.swapaxes(kv, 3, 4) for kv in stacked_kv)
    else:
        k_all, v_all = jnp.stack(new_k), jnp.stack(new_v)
    return x, k_all, v_all, jnp.stack(new_a), jnp.stack(new_c)


def kernel(x_prompt, x_sample, cache_sb_k, cache_sb_v, state_conv_a, state_conv_c, meta_tokens,
           w_in_even, conv_a_w, w_out_even, w_in_odd, conv_c_w, conv_c_b, ln_c_g, ln_c_b,
           w_out_odd, post_ln_g, post_ln_b):
    weights = (w_in_even.astype(BF16), conv_a_w, w_out_even.astype(BF16), w_in_odd.astype(BF16),
               conv_c_w, conv_c_b, ln_c_g, ln_c_b, w_out_odd.astype(BF16), post_ln_g, post_ln_b)
    b_p = x_prompt.shape[0]
    n_even = w_in_even.shape[0]
    n_odd = w_in_odd.shape[0]
    meta = jnp.broadcast_to(meta_tokens.astype(x_prompt.dtype)[None], (b_p, N_META, D_MODEL))
    xp = jnp.concatenate([meta, x_prompt], axis=1)
    zeros_a = jnp.zeros((n_even, b_p, A_CONV_W - 1, A_WIDTH), x_prompt.dtype)
    zeros_c = jnp.zeros((n_odd, b_p, C_CONV_W - 1, C_WIDTH), x_prompt.dtype)
    h_p, k_p, v_p, a_p, c_p = _trunk(xp, zeros_a, zeros_c, None, None, weights, "p")
    y_sample, k_s, v_s, a_s, c_s = _trunk(x_sample, state_conv_a, state_conv_c, cache_sb_k,
                                          cache_sb_v, weights, "s")
    return (h_p[:, N_META:], y_sample, k_p, v_p, a_p, c_p, k_s, v_s, a_s, c_s)
```

```python
import functools
import math

import jax
import jax.numpy as jnp
from jax import lax
from jax.experimental import pallas as pl
from jax.experimental.pallas import tpu as pltpu

F32 = jnp.float32
BF16 = jnp.bfloat16

D_MODEL = 1024
DEPTH = 4
N_META = 16
HEADS = 16
HEAD_DIM = 64
A_WIDTH = D_MODEL
A_CONV_W = 3
C_WIDTH = 2 * D_MODEL
C_CONV_W = 31
EVEN_IN = 8 * D_MODEL
ODD_IN = 3 * C_WIDTH
ALPHA = (2 * DEPTH) ** 0.25
LN_EPS = 1e-5
Q_SCALE = 1.0 / math.sqrt(HEAD_DIM)

LANES = 128
KEY_BLOCK = 128
VMEM_LIMIT = 56 * 1024 * 1024


def _params(n_axes):
    return pltpu.CompilerParams(dimension_semantics=("arbitrary",) * n_axes,
                                vmem_limit_bytes=VMEM_LIMIT)


def _proj_kernel(x_ref, w_ref, o_ref, xb_ref):
    @pl.when(pl.program_id(1) == 0)
    def _():
        xb_ref[...] = x_ref[...].astype(BF16)

    o_ref[...] = jnp.dot(xb_ref[...], w_ref[...], preferred_element_type=F32)


def _proj(x2d, w_bf16, tm, tn, name):
    m, k = x2d.shape
    n = w_bf16.shape[1]
    return pl.pallas_call(
        _proj_kernel,
        grid=(m // tm, n // tn),
        in_specs=[pl.BlockSpec((tm, k), lambda i, j: (i, 0)),
                  pl.BlockSpec((k, tn), lambda i, j: (0, j))],
        out_specs=pl.BlockSpec((tm, tn), lambda i, j: (i, j)),
        out_shape=jax.ShapeDtypeStruct((m, n), F32),
        scratch_shapes=[pltpu.VMEM((tm, k), BF16)],
        compiler_params=_params(2),
        name=name,
    )(x2d, w_bf16)


def _suffix_sum_matrix():
    r = lax.broadcasted_iota(jnp.int32, (2 * KEY_BLOCK, 2 * KEY_BLOCK), 0) & (KEY_BLOCK - 1)
    c = lax.broadcasted_iota(jnp.int32, (2 * KEY_BLOCK, 2 * KEY_BLOCK), 1)
    return jnp.where((c >= KEY_BLOCK) | (r > c), 1.0, 0.0).astype(BF16)


CHUNK = 2 * KEY_BLOCK
MASKED_LOGIT = -1e30


def _chunk_suffix_matrix():
    r = lax.broadcasted_iota(jnp.int32, (CHUNK, CHUNK + KEY_BLOCK), 0)
    c = lax.broadcasted_iota(jnp.int32, (CHUNK, CHUNK + KEY_BLOCK), 1)
    return jnp.where((c >= CHUNK) | (r > c), 1.0, 0.0).astype(BF16)


def _sb_scores(q, kt, msuf, mask):
    z = jnp.dot(q, kt, preferred_element_type=F32)
    lse = jnp.log(1.0 + jnp.exp(-jnp.abs(z)))
    log_beta = jnp.minimum(z, 0.0) - lse
    log_1m = log_beta - z
    if mask is not None:
        log_1m = jnp.where(mask, log_1m, 0.0)
    c = jnp.dot(log_1m.astype(BF16), msuf, preferred_element_type=F32)
    s = log_beta + c[:, :CHUNK]
    if mask is not None:
        s = jnp.where(mask, s, MASKED_LOGIT)
    return s, c[:, CHUNK:]


def _sb_weights(s, later, rows):
    if later is not None:
        s = s + jnp.concatenate([later, later], axis=1)
    a = jnp.exp(s).astype(BF16)
    return jnp.concatenate([a[:rows], a[rows:]], axis=1)


def _attn_prompt_kernel(q_ref, k_ref, v_ref, *rest, seq, first_layer):
    o_ref, nk_ref, nv_ref, qs, kts, vs = rest[-6:]
    chunk = CHUNK
    n_full = seq // chunk
    tail = seq - n_full * chunk
    padded = (n_full + 1) * chunk
    msuf = _chunk_suffix_matrix()
    lane = lax.broadcasted_iota(jnp.int32, (seq, LANES), 1)
    first = lane < HEAD_DIM

    k = k_ref[0]
    v = v_ref[0]
    q = q_ref[0] * Q_SCALE
    if first_layer:
        for later_slot in range(1, nk_ref.shape[0]):
            nk_ref[later_slot] = jnp.zeros(nk_ref.shape[1:], F32)
            nv_ref[later_slot] = jnp.zeros(nv_ref.shape[1:], F32)
    qs[0] = jnp.where(first, q, 0.0).astype(BF16)
    qs[1] = jnp.where(first, 0.0, q).astype(BF16)
    vs[0, 0:seq, :] = jnp.where(first, v, 0.0).astype(BF16)
    vs[1, 0:seq, :] = jnp.where(first, 0.0, v).astype(BF16)
    zpad = jnp.zeros((padded - seq, LANES), BF16)
    vs[0, seq:padded, :] = zpad
    vs[1, seq:padded, :] = zpad
    zrows = jnp.zeros((chunk - tail, LANES), F32)
    for j in range(n_full + 1):
        width = chunk if j < n_full else tail
        cols = slice(j * chunk, j * chunk + width)
        if j < n_full:
            kt = k[cols, :].T
            vt = v[cols, :].T
        else:
            kt = jnp.concatenate([k[cols, :], zrows], axis=0).T
            vt = jnp.concatenate([v[cols, :], zrows], axis=0).T
        kts[:, j * chunk:(j + 1) * chunk] = kt.astype(BF16)
        for h in range(2):
            rows_h = slice(h * HEAD_DIM, (h + 1) * HEAD_DIM)
            nk_ref[0, 0, h, :, cols] = kt[rows_h, :width]
            nv_ref[0, 0, h, :, cols] = vt[rows_h, :width]

    def q_tile(base, rows, n_before):
        row = lax.broadcasted_iota(jnp.int32, (2 * rows, chunk), 0)
        col = lax.broadcasted_iota(jnp.int32, (2 * rows, chunk), 1)
        mask = col < jnp.where(row < rows, row, row - rows)
        qq = jnp.concatenate([qs[0, pl.ds(base, rows), :], qs[1, pl.ds(base, rows), :]], axis=0)

        def weighted_values(a2, kb0, acc):
            vv = jnp.concatenate([vs[0, pl.ds(kb0, chunk), :], vs[1, pl.ds(kb0, chunk), :]], axis=0)
            return acc + jnp.dot(a2, vv, preferred_element_type=F32)

        s, later = _sb_scores(qq, kts[:, pl.ds(base, chunk)], msuf, mask)
        a2 = _sb_weights(s, None, rows)

        def body(i, carry):
            a2, later, acc = carry
            kb0 = pl.multiple_of((n_before - 1 - i) * chunk, chunk)
            acc = weighted_values(a2, kb0 + chunk, acc)
            s, tot = _sb_scores(qq, kts[:, pl.ds(kb0, chunk)], msuf, None)
            return _sb_weights(s, later, rows), later + tot, acc

        a2, _, acc = lax.fori_loop(0, n_before, body, (a2, later, jnp.zeros((rows, LANES), F32)))
        o_ref[0, pl.ds(base, rows), :] = weighted_values(a2, 0, acc)

    def full_tile(m, carry):
        q_tile(pl.multiple_of(m * chunk, chunk), chunk, m)
        return carry

    lax.fori_loop(0, n_full, full_tile, 0)
    if tail:
        q_tile(n_full * chunk, tail, n_full)


def _attn_prompt(p3, slot, n_slots, stacked_kv, name):
    b, t, _ = p3.shape
    padded = (t // (2 * KEY_BLOCK) + 1) * 2 * KEY_BLOCK
    col0 = 4 * A_WIDTH // LANES
    per = D_MODEL // LANES
    first = stacked_kv is None
    assert first == (slot == 0)

    def col_spec(which):
        return pl.BlockSpec((1, t, LANES), lambda i, j: (i, 0, col0 + which * per + j))

    kv_shape = jax.ShapeDtypeStruct((n_slots, b, HEADS, HEAD_DIM, t), F32)
    if first:
        kv_spec = pl.BlockSpec((n_slots, 1, 2, HEAD_DIM, t), lambda i, j: (0, i, j, 0, 0))
        extra_specs, extra_args, aliases = [], (), {}
    else:
        kv_spec = pl.BlockSpec((1, 1, 2, HEAD_DIM, t), lambda i, j: (slot, i, j, 0, 0))
        extra_specs = [pl.BlockSpec(memory_space=pl.ANY)] * 2
        extra_args = tuple(stacked_kv)
        aliases = {3: 1, 4: 2}
    return pl.pallas_call(
        functools.partial(_attn_prompt_kernel, seq=t, first_layer=first),
        grid=(b, HEADS // 2),
        in_specs=[col_spec(0), col_spec(1), col_spec(2)] + extra_specs,
        out_specs=[pl.BlockSpec((1, t, LANES), lambda i, j: (i, 0, j)), kv_spec, kv_spec],
        out_shape=[jax.ShapeDtypeStruct((b, t, D_MODEL), F32), kv_shape, kv_shape],
        scratch_shapes=[pltpu.VMEM((2, t, LANES), BF16),
                        pltpu.VMEM((LANES, padded), BF16),
                        pltpu.VMEM((2, padded, LANES), BF16)],
        input_output_aliases=aliases,
        compiler_params=_params(2),
        name=name,
    )(p3, p3, p3, *extra_args)


def _attn_sample_kernel(q_ref, k_ref, v_ref, ckt_ref, cvt_ref, ckl_ref, cvl_ref, o_ref, nk_ref, nv_ref,
                        *, past, new):
    n_blocks = (past + new + KEY_BLOCK - 1) // KEY_BLOCK
    total = n_blocks * KEY_BLOCK
    last = (n_blocks - 1) * KEY_BLOCK
    m2 = _suffix_sum_matrix()
    row = lax.broadcasted_iota(jnp.int32, (new, KEY_BLOCK), 0)
    col = lax.broadcasted_iota(jnp.int32, (new, KEY_BLOCK), 1)
    mask = col + last < row + past
    zero_rows = jnp.zeros((total - past - new, HEAD_DIM), F32)
    nt_dims = (((1,), (1,)), ((), ()))

    q2 = q_ref[0] * Q_SCALE
    k2 = k_ref[0]
    v2 = v_ref[0]
    outs = []
    for h in range(2):
        lanes = slice(h * HEAD_DIM, (h + 1) * HEAD_DIM)
        q = q2[:, lanes].astype(BF16)
        kn = k2[:, lanes]
        vn = v2[:, lanes]
        nk_ref[0, h] = kn
        nv_ref[0, h] = vn
        k_last = jnp.concatenate([ckl_ref[0, 0, h], kn, zero_rows], axis=0).astype(BF16)
        v_last = jnp.concatenate([cvl_ref[0, 0, h], vn, zero_rows], axis=0).astype(BF16)
        z = jnp.concatenate(
            [jnp.dot(q, ckt_ref[0, 0, h].astype(BF16), preferred_element_type=F32),
             lax.dot_general(q, k_last, nt_dims, preferred_element_type=F32)], axis=1)
        lse = jnp.log(1.0 + jnp.exp(-jnp.abs(z)))
        log_beta = jnp.minimum(z, 0.0) - lse
        log_1m = log_beta - z
        blocks = [log_1m[:, j * KEY_BLOCK:(j + 1) * KEY_BLOCK] for j in range(n_blocks)]
        blocks[-1] = jnp.where(mask, blocks[-1], 0.0)
        stacked = jnp.concatenate(blocks, axis=0)
        hi = stacked.astype(BF16)
        lo = (stacked - hi.astype(F32)).astype(BF16)
        c = jnp.dot(jnp.concatenate([hi, lo], axis=1), m2, preferred_element_type=F32)
        later = jnp.zeros((new, KEY_BLOCK), F32)
        between = [None] * n_blocks
        for j in reversed(range(n_blocks)):
            cj = c[j * new:(j + 1) * new]
            between[j] = cj[:, :KEY_BLOCK] + later
            later = later + cj[:, KEY_BLOCK:]
        a = jnp.exp(log_beta + jnp.concatenate(between, axis=1))
        a_last = jnp.where(mask, a[:, last:], 0.0).astype(BF16)
        out = lax.dot_general(a[:, :last].astype(BF16), cvt_ref[0, 0, h].astype(BF16), nt_dims,
                              preferred_element_type=F32)
        outs.append(out + jnp.dot(a_last, v_last, preferred_element_type=F32))
    o_ref[0] = jnp.concatenate(outs, axis=1)


def _attn_sample(p3, cache_t, cache_last, layer, name):
    b, new, _ = p3.shape
    past = cache_t[0].shape[4]
    last = past // KEY_BLOCK * KEY_BLOCK
    assert 0 < past - last and past - last + new <= KEY_BLOCK
    col0 = 4 * A_WIDTH // LANES
    per = D_MODEL // LANES

    def col_spec(which):
        return pl.BlockSpec((1, new, LANES), lambda i, j: (i, 0, col0 + which * per + j))

    cache_spec = pl.BlockSpec((1, 1, 2, HEAD_DIM, last), lambda i, j: (layer, i, j, 0, 0))
    last_spec = pl.BlockSpec((1, 1, 2, past - last, HEAD_DIM), lambda i, j: (layer, i, j, 0, 0))
    kv_shape = jax.ShapeDtypeStruct((b, HEADS, new, HEAD_DIM), F32)
    kv_spec = pl.BlockSpec((1, 2, new, HEAD_DIM), lambda i, j: (i, j, 0, 0))
    return pl.pallas_call(
        functools.partial(_attn_sample_kernel, past=past, new=new),
        grid=(b, HEADS // 2),
        in_specs=[col_spec(0), col_spec(1), col_spec(2), cache_spec, cache_spec, last_spec, last_spec],
        out_specs=[pl.BlockSpec((1, new, LANES), lambda i, j: (i, 0, j)), kv_spec, kv_spec],
        out_shape=[jax.ShapeDtypeStruct((b, new, D_MODEL), F32), kv_shape, kv_shape],
        compiler_params=_params(2),
        name=name,
    )(p3, p3, p3, *cache_t, *cache_last)


def _sigmoid(x):
    return 0.5 * jnp.tanh(0.5 * x) + 0.5


def _silu(x):
    return x * _sigmoid(x)


def _layer_norm(x, g, b):
    mu = jnp.mean(x, axis=-1, keepdims=True)
    d = x - mu
    var = jnp.mean(d * d, axis=-1, keepdims=True)
    return d * lax.rsqrt(var + LN_EPS) * g + b


def _even_mix_kernel(x_ref, h_ref, gb_ref, gc_ref, za_ref, zb_ref, o_ref, buf_ref, cw_ref, wo_ref,
                     g_ref, b_ref, y_ref, nbuf_ref, ext, *, tm):
    hist = A_CONV_W - 1
    pad = 8

    @pl.when(pl.program_id(1) == 0)
    def _():
        ext[pad - hist:pad, :] = buf_ref[0]

    @pl.when(pl.program_id(1) > 0)
    def _():
        ext[0:pad, :] = ext[tm:tm + pad, :]

    u = gc_ref[0] * h_ref[0]
    ext[pad:pad + tm, :] = u
    cw = cw_ref[...]
    conv = cw[2:3, :] * u
    for tap in range(hist):
        conv = conv + cw[tap:tap + 1, :] * ext[pad - hist + tap:pad - hist + tap + tm, :]
    nbuf_ref[0] = ext[tm + pad - hist:tm + pad, :]
    y_a = gb_ref[0] * conv * _silu(za_ref[0])
    y_b = o_ref[0] * _silu(zb_ref[0])
    out = jnp.dot(y_a.astype(BF16), wo_ref[0:A_WIDTH, :], preferred_element_type=F32)
    out = out + jnp.dot(y_b.astype(BF16), wo_ref[A_WIDTH:, :], preferred_element_type=F32)
    y_ref[0] = _layer_norm(ALPHA * x_ref[0] + out, g_ref[...], b_ref[...])


def _even_mix(x, p3, o, conv_buf, conv_w, w_out_bf16, ln_g, ln_b, tm, name):
    b, t, d = x.shape
    hist = A_CONV_W - 1

    def row_spec(col):
        return pl.BlockSpec((1, tm, d), lambda i, j: (i, j, col))

    def full_spec(shape):
        return pl.BlockSpec(shape, lambda i, j: (0,) * len(shape))

    buf_spec = pl.BlockSpec((1, hist, A_WIDTH), lambda i, j: (i, 0, 0))
    return pl.pallas_call(
        functools.partial(_even_mix_kernel, tm=tm),
        grid=(b, t // tm),
        in_specs=[row_spec(0), row_spec(0), row_spec(1), row_spec(2), row_spec(3), row_spec(7),
                  row_spec(0), buf_spec, full_spec((A_CONV_W, A_WIDTH)),
                  full_spec((2 * D_MODEL, D_MODEL)), full_spec((1, d)), full_spec((1, d))],
        out_specs=[row_spec(0), buf_spec],
        out_shape=[jax.ShapeDtypeStruct((b, t, d), F32),
                   jax.ShapeDtypeStruct((b, hist, A_WIDTH), F32)],
        scratch_shapes=[pltpu.VMEM((tm + 8, A_WIDTH), F32)],
        compiler_params=_params(2),
        name=name,
    )(x, p3, p3, p3, p3, p3, o, conv_buf, conv_w, w_out_bf16, ln_g[None], ln_b[None])


def _odd_mix_kernel(x_ref, a_ref, gate_ref, zc_ref, buf_ref, cw_ref, cb_ref, cg_ref, cbeta_ref, wo_ref,
                    g_ref, b_ref, y_ref, nbuf_ref, ext, conv_s, shifted, *, tm):
    hist = C_CONV_W - 1
    pad = 32

    @pl.when(pl.program_id(1) == 0)
    def _():
        ext[pad - hist:pad, :] = buf_ref[0]

    @pl.when(pl.program_id(1) > 0)
    def _():
        ext[0:pad, :] = ext[tm:tm + pad, :]

    ext[pad:pad + tm, :] = a_ref[0] * _sigmoid(gate_ref[0])
    nbuf_ref[0] = ext[tm + pad - hist:tm + pad, :]

    sub = 8

    def lane_chunk(c, carry):
        lanes = pl.ds(pl.multiple_of(c * LANES, LANES), LANES)
        for phase in range(sub):
            span = tm + pad if phase == 0 else tm + pad - sub
            shifted[phase, 0:span, :] = ext[phase:phase + span, lanes]
        acc = jnp.zeros((tm, LANES), F32) + cb_ref[:, lanes]
        for tap in range(C_CONV_W):
            first_row = pad - hist + tap
            phase = first_row % sub
            acc = acc + cw_ref[tap:tap + 1, lanes] * shifted[phase, first_row - phase:first_row - phase + tm, :]
        conv_s[:, lanes] = acc
        return carry

    lax.fori_loop(0, C_WIDTH // LANES, lane_chunk, 0)
    y = _silu(_layer_norm(conv_s[...], cg_ref[...], cbeta_ref[...])) * _silu(zc_ref[0])
    out = jnp.dot(y.astype(BF16), wo_ref[...], preferred_element_type=F32)
    y_ref[0] = _layer_norm(ALPHA * x_ref[0] + out, g_ref[...], b_ref[...])


def _odd_mix(x, p3, conv_buf, conv_w, conv_b, ln_c_g, ln_c_b, w_out_bf16, ln_g, ln_b, tm, name):
    b, t, d = x.shape
    hist = C_CONV_W - 1

    def row_spec(width, col):
        return pl.BlockSpec((1, tm, width), lambda i, j: (i, j, col))

    def full_spec(shape):
        return pl.BlockSpec(shape, lambda i, j: (0,) * len(shape))

    buf_spec = pl.BlockSpec((1, hist, C_WIDTH), lambda i, j: (i, 0, 0))
    return pl.pallas_call(
        functools.partial(_odd_mix_kernel, tm=tm),
        grid=(b, t // tm),
        in_specs=[row_spec(d, 0), row_spec(C_WIDTH, 0), row_spec(C_WIDTH, 1), row_spec(C_WIDTH, 2),
                  buf_spec, full_spec((C_CONV_W, C_WIDTH)), full_spec((1, C_WIDTH)),
                  full_spec((1, C_WIDTH)), full_spec((1, C_WIDTH)), full_spec((C_WIDTH, D_MODEL)),
                  full_spec((1, d)), full_spec((1, d))],
        out_specs=[row_spec(d, 0), buf_spec],
        out_shape=[jax.ShapeDtypeStruct((b, t, d), F32),
                   jax.ShapeDtypeStruct((b, hist, C_WIDTH), F32)],
        scratch_shapes=[pltpu.VMEM((tm + 32, C_WIDTH), F32), pltpu.VMEM((tm, C_WIDTH), F32),
                        pltpu.VMEM((8, tm + 32, LANES), F32)],
        compiler_params=_params(2),
        name=name,
    )(x, p3, p3, p3, conv_buf, conv_w, conv_b[None], ln_c_g[None], ln_c_b[None], w_out_bf16,
      ln_g[None], ln_b[None])


PROMPT_ROW_TILE = 344
PROJ_COL_TILE = 512


def _trunk(x, conv_a_bufs, conv_c_bufs, cache_k, cache_v, weights, tag):
    (w_in_even, conv_a_w, w_out_even, w_in_odd, conv_c_w, conv_c_b, ln_c_g, ln_c_b, w_out_odd,
     post_ln_g, post_ln_b) = weights
    b, t, d = x.shape
    prompt = cache_k is None
    if not prompt:
        whole = cache_k.shape[3] // KEY_BLOCK * KEY_BLOCK
        cache_t = (jnp.swapaxes(cache_k, 3, 4), jnp.swapaxes(cache_v, 3, 4))
        cache_last = (cache_k[:, :, :, whole:, :], cache_v[:, :, :, whole:, :])
    proj_tm = t if prompt else b * t
    mix_tm = PROMPT_ROW_TILE if prompt else t
    n_even = w_in_even.shape[0]
    new_k, new_v, new_a, new_c = [], [], [], []
    stacked_kv = None
    for layer in range(DEPTH):
        i = layer // 2
        nm = f"{tag}{layer}"
        if layer % 2 == 0:
            p3 = _proj(x.reshape(b * t, d), w_in_even[i], proj_tm, PROJ_COL_TILE,
                       f"proj_{nm}").reshape(b, t, EVEN_IN)
            if prompt:
                o, *stacked_kv = _attn_prompt(p3, i, n_even, stacked_kv, f"attn_{nm}")
            else:
                o, kh, vh = _attn_sample(p3, cache_t, cache_last, i, f"attn_{nm}")
                new_k.append(kh)
                new_v.append(vh)
            x, buf = _even_mix(x, p3, o, conv_a_bufs[i], conv_a_w[i], w_out_even[i],
                               post_ln_g[layer], post_ln_b[layer], mix_tm, f"mix_{nm}")
            new_a.append(buf)
        else:
            p3 = _proj(x.reshape(b * t, d), w_in_odd[i], proj_tm, PROJ_COL_TILE,
                       f"proj_{nm}").reshape(b, t, ODD_IN)
            x, buf = _odd_mix(x, p3, conv_c_bufs[i], conv_c_w[i], conv_c_b[i], ln_c_g[i], ln_c_b[i],
                              w_out_odd[i], post_ln_g[layer], post_ln_b[layer], mix_tm, f"mix_{nm}")
            new_c.append(buf)
    if prompt:
        k_all, v_all = (jnp.swapaxes(kv, 3, 4) for kv in stacked_kv)
    else:
        k_all, v_all = jnp.stack(new_k), jnp.stack(new_v)
    return x, k_all, v_all, jnp.stack(new_a), jnp.stack(new_c)


def kernel(x_prompt, x_sample, cache_sb_k, cache_sb_v, state_conv_a, state_conv_c, meta_tokens,
           w_in_even, conv_a_w, w_out_even, w_in_odd, conv_c_w, conv_c_b, ln_c_g, ln_c_b,
           w_out_odd, post_ln_g, post_ln_b):
    weights = (w_in_even.astype(BF16), conv_a_w, w_out_even.astype(BF16), w_in_odd.astype(BF16),
               conv_c_w, conv_c_b, ln_c_g, ln_c_b, w_out_odd.astype(BF16), post_ln_g, post_ln_b)
    b_p = x_prompt.shape[0]
    n_even = w_in_even.shape[0]
    n_odd = w_in_odd.shape[0]
    meta = jnp.broadcast_to(meta_tokens.astype(x_prompt.dtype)[None], (b_p, N_META, D_MODEL))
    xp = jnp.concatenate([meta, x_prompt], axis=1)
    zeros_a = jnp.zeros((n_even, b_p, A_CONV_W - 1, A_WIDTH), x_prompt.dtype)
    zeros_c = jnp.zeros((n_odd, b_p, C_CONV_W - 1, C_WIDTH), x_prompt.dtype)
    h_p, k_p, v_p, a_p, c_p = _trunk(xp, zeros_a, zeros_c, None, None, weights, "p")
    y_sample, k_s, v_s, a_s, c_s = _trunk(x_sample, state_conv_a, state_conv_c, cache_sb_k,
                                          cache_sb_v, weights, "s")
    return (h_p[:, N_META:], y_sample, k_p, v_p, a_p, c_p, k_s, v_s, a_s, c_s)
```

```python
import functools
import math

import jax
import jax.numpy as jnp
from jax import lax
from jax.experimental import pallas as pl
from jax.experimental.pallas import tpu as pltpu

F32 = jnp.float32
BF16 = jnp.bfloat16

D_MODEL = 1024
DEPTH = 4
N_META = 16
HEADS = 16
HEAD_DIM = 64
A_WIDTH = D_MODEL
A_CONV_W = 3
C_WIDTH = 2 * D_MODEL
C_CONV_W = 31
EVEN_IN = 8 * D_MODEL
ODD_IN = 3 * C_WIDTH
ALPHA = (2 * DEPTH) ** 0.25
LN_EPS = 1e-5
Q_SCALE = 1.0 / math.sqrt(HEAD_DIM)

LANES = 128
KEY_BLOCK = 128
VMEM_LIMIT = 56 * 1024 * 1024


def _params(n_axes):
    return pltpu.CompilerParams(dimension_semantics=("arbitrary",) * n_axes,
                                vmem_limit_bytes=VMEM_LIMIT)


def _proj_kernel(x_ref, w_ref, o_ref, xb_ref):
    @pl.when(pl.program_id(1) == 0)
    def _():
        xb_ref[...] = x_ref[...].astype(BF16)

    o_ref[...] = jnp.dot(xb_ref[...], w_ref[...], preferred_element_type=F32)


def _proj(x2d, w_bf16, tm, tn, name):
    m, k = x2d.shape
    n = w_bf16.shape[1]
    return pl.pallas_call(
        _proj_kernel,
        grid=(m // tm, n // tn),
        in_specs=[pl.BlockSpec((tm, k), lambda i, j: (i, 0)),
                  pl.BlockSpec((k, tn), lambda i, j: (0, j))],
        out_specs=pl.BlockSpec((tm, tn), lambda i, j: (i, j)),
        out_shape=jax.ShapeDtypeStruct((m, n), F32),
        scratch_shapes=[pltpu.VMEM((tm, k), BF16)],
        compiler_params=_params(2),
        name=name,
    )(x2d, w_bf16)


def _suffix_sum_matrix():
    r = lax.broadcasted_iota(jnp.int32, (2 * KEY_BLOCK, 2 * KEY_BLOCK), 0) & (KEY_BLOCK - 1)
    c = lax.broadcasted_iota(jnp.int32, (2 * KEY_BLOCK, 2 * KEY_BLOCK), 1)
    return jnp.where((c >= KEY_BLOCK) | (r > c), 1.0, 0.0).astype(BF16)


CHUNK = 2 * KEY_BLOCK
MASKED_LOGIT = -1e30


def _chunk_suffix_matrix():
    r = lax.broadcasted_iota(jnp.int32, (CHUNK, CHUNK + KEY_BLOCK), 0)
    c = lax.broadcasted_iota(jnp.int32, (CHUNK, CHUNK + KEY_BLOCK), 1)
    return jnp.where((c >= CHUNK) | (r > c), 1.0, 0.0).astype(BF16)


def _sb_scores(q, kt, msuf, mask):
    z = jnp.dot(q, kt, preferred_element_type=F32)
    lse = jnp.log(1.0 + jnp.exp(-jnp.abs(z)))
    log_beta = jnp.minimum(z, 0.0) - lse
    log_1m = log_beta - z
    if mask is not None:
        log_1m = jnp.where(mask, log_1m, 0.0)
    c = jnp.dot(log_1m.astype(BF16), msuf, preferred_element_type=F32)
    s = log_beta + c[:, :CHUNK]
    if mask is not None:
        s = jnp.where(mask, s, MASKED_LOGIT)
    return s, c[:, CHUNK:]


def _sb_weights(s, later, rows):
    if later is not None:
        s = s + jnp.concatenate([later, later], axis=1)
    a = jnp.exp(s).astype(BF16)
    return jnp.concatenate([a[:rows], a[rows:]], axis=1)


def _attn_prompt_kernel(q_ref, k_ref, v_ref, *rest, seq, first_layer):
    o_ref, nk_ref, nv_ref, qs, kts, vs = rest[-6:]
    chunk = CHUNK
    n_full = seq // chunk
    tail = seq - n_full * chunk
    padded = (n_full + 1) * chunk
    msuf = _chunk_suffix_matrix()
    lane = lax.broadcasted_iota(jnp.int32, (seq, LANES), 1)
    first = lane < HEAD_DIM

    k = k_ref[0]
    v = v_ref[0]
    q = q_ref[0] * Q_SCALE
    if first_layer:
        for later_slot in range(1, nk_ref.shape[0]):
            nk_ref[later_slot] = jnp.zeros(nk_ref.shape[1:], F32)
            nv_ref[later_slot] = jnp.zeros(nv_ref.shape[1:], F32)
    qs[0] = jnp.where(first, q, 0.0).astype(BF16)
    qs[1] = jnp.where(first, 0.0, q).astype(BF16)
    vs[0, 0:seq, :] = jnp.where(first, v, 0.0).astype(BF16)
    vs[1, 0:seq, :] = jnp.where(first, 0.0, v).astype(BF16)
    zpad = jnp.zeros((padded - seq, LANES), BF16)
    vs[0, seq:padded, :] = zpad
    vs[1, seq:padded, :] = zpad
    zrows = jnp.zeros((chunk - tail, LANES), F32)
    for j in range(n_full + 1):
        width = chunk if j < n_full else tail
        cols = slice(j * chunk, j * chunk + width)
        if j < n_full:
            kt = k[cols, :].T
            vt = v[cols, :].T
        else:
            kt = jnp.concatenate([k[cols, :], zrows], axis=0).T
            vt = jnp.concatenate([v[cols, :], zrows], axis=0).T
        kts[:, j * chunk:(j + 1) * chunk] = kt.astype(BF16)
        for h in range(2):
            rows_h = slice(h * HEAD_DIM, (h + 1) * HEAD_DIM)
            nk_ref[0, 0, h, :, cols] = kt[rows_h, :width]
            nv_ref[0, 0, h, :, cols] = vt[rows_h, :width]

    def q_tile(base, rows, n_before, extra_rows=0):
        all_rows = rows + extra_rows

        def weighted_values(a2, kb0, acc):
            vv = jnp.concatenate([vs[0, pl.ds(kb0, chunk), :], vs[1, pl.ds(kb0, chunk), :]], axis=0)
            return acc + jnp.dot(a2, vv, preferred_element_type=F32)

        def causal(n):
            return (lax.broadcasted_iota(jnp.int32, (n, chunk), 1)
                    < lax.broadcasted_iota(jnp.int32, (n, chunk), 0))

        q_parts = [qs[h, pl.ds(base, rows), :] for h in range(2)]
        if extra_rows:
            ebase = base + chunk
            e_parts = [qs[h, pl.ds(ebase, extra_rows), :] for h in range(2)]
            emask = jnp.concatenate([causal(extra_rows)] * 2, axis=0)
            s_e, tot_e = _sb_scores(jnp.concatenate(e_parts, axis=0), kts[:, pl.ds(ebase, chunk)], msuf, emask)
            acc_e = weighted_values(_sb_weights(s_e, None, extra_rows), ebase,
                                    jnp.zeros((extra_rows, LANES), F32))
            none_right = jnp.zeros((rows, KEY_BLOCK), F32)
            later0 = jnp.concatenate([none_right, tot_e[:extra_rows], none_right, tot_e[extra_rows:]], axis=0)
            acc0 = jnp.concatenate([jnp.zeros((rows, LANES), F32), acc_e], axis=0)
            qq = jnp.concatenate([q_parts[0], e_parts[0], q_parts[1], e_parts[1]], axis=0)
            own = causal(all_rows) | (lax.broadcasted_iota(jnp.int32, (all_rows, chunk), 0) >= rows)
        else:
            later0 = None
            acc0 = jnp.zeros((rows, LANES), F32)
            qq = jnp.concatenate(q_parts, axis=0)
            own = causal(rows)
        mask = jnp.concatenate([own, own], axis=0)

        s, tot = _sb_scores(qq, kts[:, pl.ds(base, chunk)], msuf, mask)
        a2 = _sb_weights(s, later0, all_rows)
        later = tot if later0 is None else later0 + tot

        def body(i, carry):
            a2, later, acc = carry
            kb0 = pl.multiple_of((n_before - 1 - i) * chunk, chunk)
            acc = weighted_values(a2, kb0 + chunk, acc)
            s, tot = _sb_scores(qq, kts[:, pl.ds(kb0, chunk)], msuf, None)
            return _sb_weights(s, later, all_rows), later + tot, acc

        a2, _, acc = lax.fori_loop(0, n_before, body, (a2, later, acc0))
        out = weighted_values(a2, 0, acc)
        o_ref[0, pl.ds(base, rows), :] = out[:rows]
        if extra_rows:
            o_ref[0, pl.ds(base + chunk, extra_rows), :] = out[rows:]

    def full_tile(m, carry):
        q_tile(pl.multiple_of(m * chunk, chunk), chunk, m)
        return carry

    assert n_full >= 1
    lax.fori_loop(0, n_full - 1 if tail else n_full, full_tile, 0)
    if tail:
        q_tile((n_full - 1) * chunk, chunk, n_full - 1, extra_rows=tail)


def _attn_prompt(p3, slot, n_slots, stacked_kv, name):
    b, t, _ = p3.shape
    padded = (t // (2 * KEY_BLOCK) + 1) * 2 * KEY_BLOCK
    col0 = 4 * A_WIDTH // LANES
    per = D_MODEL // LANES
    first = stacked_kv is None
    assert first == (slot == 0)

    def col_spec(which):
        return pl.BlockSpec((1, t, LANES), lambda i, j: (i, 0, col0 + which * per + j))

    kv_shape = jax.ShapeDtypeStruct((n_slots, b, HEADS, HEAD_DIM, t), F32)
    if first:
        kv_spec = pl.BlockSpec((n_slots, 1, 2, HEAD_DIM, t), lambda i, j: (0, i, j, 0, 0))
        extra_specs, extra_args, aliases = [], (), {}
    else:
        kv_spec = pl.BlockSpec((1, 1, 2, HEAD_DIM, t), lambda i, j: (slot, i, j, 0, 0))
        extra_specs = [pl.BlockSpec(memory_space=pl.ANY)] * 2
        extra_args = tuple(stacked_kv)
        aliases = {3: 1, 4: 2}
    return pl.pallas_call(
        functools.partial(_attn_prompt_kernel, seq=t, first_layer=first),
        grid=(b, HEADS // 2),
        in_specs=[col_spec(0), col_spec(1), col_spec(2)] + extra_specs,
        out_specs=[pl.BlockSpec((1, t, LANES), lambda i, j: (i, 0, j)), kv_spec, kv_spec],
        out_shape=[jax.ShapeDtypeStruct((b, t, D_MODEL), F32), kv_shape, kv_shape],
        scratch_shapes=[pltpu.VMEM((2, t, LANES), BF16),
                        pltpu.VMEM((LANES, padded), BF16),
                        pltpu.VMEM((2, padded, LANES), BF16)],
        input_output_aliases=aliases,
        compiler_params=_params(2),
        name=name,
    )(p3, p3, p3, *extra_args)


def _attn_sample_kernel(q_ref, k_ref, v_ref, ckt_ref, cvt_ref, ckl_ref, cvl_ref, o_ref, nk_ref, nv_ref,
                        *, past, new):
    n_blocks = (past + new + KEY_BLOCK - 1) // KEY_BLOCK
    total = n_blocks * KEY_BLOCK
    last = (n_blocks - 1) * KEY_BLOCK
    m2 = _suffix_sum_matrix()
    row = lax.broadcasted_iota(jnp.int32, (new, KEY_BLOCK), 0)
    col = lax.broadcasted_iota(jnp.int32, (new, KEY_BLOCK), 1)
    mask = col + last < row + past
    zero_rows = jnp.zeros((total - past - new, HEAD_DIM), F32)
    nt_dims = (((1,), (1,)), ((), ()))

    q2 = q_ref[0] * Q_SCALE
    k2 = k_ref[0]
    v2 = v_ref[0]
    outs = []
    for h in range(2):
        lanes = slice(h * HEAD_DIM, (h + 1) * HEAD_DIM)
        q = q2[:, lanes].astype(BF16)
        kn = k2[:, lanes]
        vn = v2[:, lanes]
        nk_ref[0, h] = kn
        nv_ref[0, h] = vn
        k_last = jnp.concatenate([ckl_ref[0, 0, h], kn, zero_rows], axis=0).astype(BF16)
        v_last = jnp.concatenate([cvl_ref[0, 0, h], vn, zero_rows], axis=0).astype(BF16)
        z = jnp.concatenate(
            [jnp.dot(q, ckt_ref[0, 0, h].astype(BF16), preferred_element_type=F32),
             lax.dot_general(q, k_last, nt_dims, preferred_element_type=F32)], axis=1)
        lse = jnp.log(1.0 + jnp.exp(-jnp.abs(z)))
        log_beta = jnp.minimum(z, 0.0) - lse
        log_1m = log_beta - z
        blocks = [log_1m[:, j * KEY_BLOCK:(j + 1) * KEY_BLOCK] for j in range(n_blocks)]
        blocks[-1] = jnp.where(mask, blocks[-1], 0.0)
        stacked = jnp.concatenate(blocks, axis=0)
        hi = stacked.astype(BF16)
        lo = (stacked - hi.astype(F32)).astype(BF16)
        c = jnp.dot(jnp.concatenate([hi, lo], axis=1), m2, preferred_element_type=F32)
        later = jnp.zeros((new, KEY_BLOCK), F32)
        between = [None] * n_blocks
        for j in reversed(range(n_blocks)):
            cj = c[j * new:(j + 1) * new]
            between[j] = cj[:, :KEY_BLOCK] + later
            later = later + cj[:, KEY_BLOCK:]
        a = jnp.exp(log_beta + jnp.concatenate(between, axis=1))
        a_last = jnp.where(mask, a[:, last:], 0.0).astype(BF16)
        out = lax.dot_general(a[:, :last].astype(BF16), cvt_ref[0, 0, h].astype(BF16), nt_dims,
                              preferred_element_type=F32)
        outs.append(out + jnp.dot(a_last, v_last, preferred_element_type=F32))
    o_ref[0] = jnp.concatenate(outs, axis=1)


def _attn_sample(p3, cache_t, cache_last, layer, name):
    b, new, _ = p3.shape
    past = cache_t[0].shape[4]
    last = past // KEY_BLOCK * KEY_BLOCK
    assert 0 < past - last and past - last + new <= KEY_BLOCK
    col0 = 4 * A_WIDTH // LANES
    per = D_MODEL // LANES

    def col_spec(which):
        return pl.BlockSpec((1, new, LANES), lambda i, j: (i, 0, col0 + which * per + j))

    cache_spec = pl.BlockSpec((1, 1, 2, HEAD_DIM, last), lambda i, j: (layer, i, j, 0, 0))
    last_spec = pl.BlockSpec((1, 1, 2, past - last, HEAD_DIM), lambda i, j: (layer, i, j, 0, 0))
    kv_shape = jax.ShapeDtypeStruct((b, HEADS, new, HEAD_DIM), F32)
    kv_spec = pl.BlockSpec((1, 2, new, HEAD_DIM), lambda i, j: (i, j, 0, 0))
    return pl.pallas_call(
        functools.partial(_attn_sample_kernel, past=past, new=new),
        grid=(b, HEADS // 2),
        in_specs=[col_spec(0), col_spec(1), col_spec(2), cache_spec, cache_spec, last_spec, last_spec],
        out_specs=[pl.BlockSpec((1, new, LANES), lambda i, j: (i, 0, j)), kv_spec, kv_spec],
        out_shape=[jax.ShapeDtypeStruct((b, new, D_MODEL), F32), kv_shape, kv_shape],
        compiler_params=_params(2),
        name=name,
    )(p3, p3, p3, *cache_t, *cache_last)


def _sigmoid(x):
    return 0.5 * jnp.tanh(0.5 * x) + 0.5


def _silu(x):
    return x * _sigmoid(x)


def _layer_norm(x, g, b):
    mu = jnp.mean(x, axis=-1, keepdims=True)
    d = x - mu
    var = jnp.mean(d * d, axis=-1, keepdims=True)
    return d * lax.rsqrt(var + LN_EPS) * g + b


def _even_mix_kernel(x_ref, h_ref, gb_ref, gc_ref, za_ref, zb_ref, o_ref, buf_ref, cw_ref, wo_ref,
                     g_ref, b_ref, y_ref, nbuf_ref, ext, *, tm):
    hist = A_CONV_W - 1
    pad = 8

    @pl.when(pl.program_id(1) == 0)
    def _():
        ext[pad - hist:pad, :] = buf_ref[0]

    @pl.when(pl.program_id(1) > 0)
    def _():
        ext[0:pad, :] = ext[tm:tm + pad, :]

    u = gc_ref[0] * h_ref[0]
    ext[pad:pad + tm, :] = u
    cw = cw_ref[...]
    conv = cw[2:3, :] * u
    for tap in range(hist):
        conv = conv + cw[tap:tap + 1, :] * ext[pad - hist + tap:pad - hist + tap + tm, :]
    nbuf_ref[0] = ext[tm + pad - hist:tm + pad, :]
    y_a = gb_ref[0] * conv * _silu(za_ref[0])
    y_b = o_ref[0] * _silu(zb_ref[0])
    out = jnp.dot(y_a.astype(BF16), wo_ref[0:A_WIDTH, :], preferred_element_type=F32)
    out = out + jnp.dot(y_b.astype(BF16), wo_ref[A_WIDTH:, :], preferred_element_type=F32)
    y_ref[0] = _layer_norm(ALPHA * x_ref[0] + out, g_ref[...], b_ref[...])


def _even_mix(x, p3, o, conv_buf, conv_w, w_out_bf16, ln_g, ln_b, tm, name):
    b, t, d = x.shape
    hist = A_CONV_W - 1

    def row_spec(col):
        return pl.BlockSpec((1, tm, d), lambda i, j: (i, j, col))

    def full_spec(shape):
        return pl.BlockSpec(shape, lambda i, j: (0,) * len(shape))

    buf_spec = pl.BlockSpec((1, hist, A_WIDTH), lambda i, j: (i, 0, 0))
    return pl.pallas_call(
        functools.partial(_even_mix_kernel, tm=tm),
        grid=(b, t // tm),
        in_specs=[row_spec(0), row_spec(0), row_spec(1), row_spec(2), row_spec(3), row_spec(7),
                  row_spec(0), buf_spec, full_spec((A_CONV_W, A_WIDTH)),
                  full_spec((2 * D_MODEL, D_MODEL)), full_spec((1, d)), full_spec((1, d))],
        out_specs=[row_spec(0), buf_spec],
        out_shape=[jax.ShapeDtypeStruct((b, t, d), F32),
                   jax.ShapeDtypeStruct((b, hist, A_WIDTH), F32)],
        scratch_shapes=[pltpu.VMEM((tm + 8, A_WIDTH), F32)],
        compiler_params=_params(2),
        name=name,
    )(x, p3, p3, p3, p3, p3, o, conv_buf, conv_w, w_out_bf16, ln_g[None], ln_b[None])


def _odd_mix_kernel(x_ref, a_ref, gate_ref, zc_ref, buf_ref, cw_ref, cb_ref, cg_ref, cbeta_ref, wo_ref,
                    g_ref, b_ref, y_ref, nbuf_ref, ext, conv_s, shifted, *, tm):
    hist = C_CONV_W - 1
    pad = 32

    @pl.when(pl.program_id(1) == 0)
    def _():
        ext[pad - hist:pad, :] = buf_ref[0]

    @pl.when(pl.program_id(1) > 0)
    def _():
        ext[0:pad, :] = ext[tm:tm + pad, :]

    ext[pad:pad + tm, :] = a_ref[0] * _sigmoid(gate_ref[0])
    nbuf_ref[0] = ext[tm + pad - hist:tm + pad, :]

    sub = 8

    def lane_chunk(c, carry):
        lanes = pl.ds(pl.multiple_of(c * LANES, LANES), LANES)
        for phase in range(sub):
            span = tm + pad if phase == 0 else tm + pad - sub
            shifted[phase, 0:span, :] = ext[phase:phase + span, lanes]
        acc = jnp.zeros((tm, LANES), F32) + cb_ref[:, lanes]
        for tap in range(C_CONV_W):
            first_row = pad - hist + tap
            phase = first_row % sub
            acc = acc + cw_ref[tap:tap + 1, lanes] * shifted[phase, first_row - phase:first_row - phase + tm, :]
        conv_s[:, lanes] = acc
        return carry

    lax.fori_loop(0, C_WIDTH // LANES, lane_chunk, 0)
    y = _silu(_layer_norm(conv_s[...], cg_ref[...], cbeta_ref[...])) * _silu(zc_ref[0])
    out = jnp.dot(y.astype(BF16), wo_ref[...], preferred_element_type=F32)
    y_ref[0] = _layer_norm(ALPHA * x_ref[0] + out, g_ref[...], b_ref[...])


def _odd_mix(x, p3, conv_buf, conv_w, conv_b, ln_c_g, ln_c_b, w_out_bf16, ln_g, ln_b, tm, name):
    b, t, d = x.shape
    hist = C_CONV_W - 1

    def row_spec(width, col):
        return pl.BlockSpec((1, tm, width), lambda i, j: (i, j, col))

    def full_spec(shape):
        return pl.BlockSpec(shape, lambda i, j: (0,) * len(shape))

    buf_spec = pl.BlockSpec((1, hist, C_WIDTH), lambda i, j: (i, 0, 0))
    return pl.pallas_call(
        functools.partial(_odd_mix_kernel, tm=tm),
        grid=(b, t // tm),
        in_specs=[row_spec(d, 0), row_spec(C_WIDTH, 0), row_spec(C_WIDTH, 1), row_spec(C_WIDTH, 2),
                  buf_spec, full_spec((C_CONV_W, C_WIDTH)), full_spec((1, C_WIDTH)),
                  full_spec((1, C_WIDTH)), full_spec((1, C_WIDTH)), full_spec((C_WIDTH, D_MODEL)),
                  full_spec((1, d)), full_spec((1, d))],
        out_specs=[row_spec(d, 0), buf_spec],
        out_shape=[jax.ShapeDtypeStruct((b, t, d), F32),
                   jax.ShapeDtypeStruct((b, hist, C_WIDTH), F32)],
        scratch_shapes=[pltpu.VMEM((tm + 32, C_WIDTH), F32), pltpu.VMEM((tm, C_WIDTH), F32),
                        pltpu.VMEM((8, tm + 32, LANES), F32)],
        compiler_params=_params(2),
        name=name,
    )(x, p3, p3, p3, conv_buf, conv_w, conv_b[None], ln_c_g[None], ln_c_b[None], w_out_bf16,
      ln_g[None], ln_b[None])


PROMPT_ROW_TILE = 344
PROJ_COL_TILE = 1024


def _trunk(x, conv_a_bufs, conv_c_bufs, cache_k, cache_v, weights, tag):
    (w_in_even, conv_a_w, w_out_even, w_in_odd, conv_c_w, conv_c_b, ln_c_g, ln_c_b, w_out_odd,
     post_ln_g, post_ln_b) = weights
    b, t, d = x.shape
    prompt = cache_k is None
    if not prompt:
        whole = cache_k.shape[3] // KEY_BLOCK * KEY_BLOCK
        cache_t = (jnp.swapaxes(cache_k, 3, 4), jnp.swapaxes(cache_v, 3, 4))
        cache_last = (cache_k[:, :, :, whole:, :], cache_v[:, :, :, whole:, :])
    proj_tm = t if prompt else b * t
    mix_tm = PROMPT_ROW_TILE if prompt else t
    n_even = w_in_even.shape[0]
    new_k, new_v, new_a, new_c = [], [], [], []
    stacked_kv = None
    for layer in range(DEPTH):
        i = layer // 2
        nm = f"{tag}{layer}"
        if layer % 2 == 0:
            p3 = _proj(x.reshape(b * t, d), w_in_even[i], proj_tm, PROJ_COL_TILE,
                       f"proj_{nm}").reshape(b, t, EVEN_IN)
            if prompt:
                o, *stacked_kv = _attn_prompt(p3, i, n_even, stacked_kv, f"attn_{nm}")
            else:
                o, kh, vh = _attn_sample(p3, cache_t, cache_last, i, f"attn_{nm}")
                new_k.append(kh)
                new_v.append(vh)
            x, buf = _even_mix(x, p3, o, conv_a_bufs[i], conv_a_w[i], w_out_even[i],
                               post_ln_g[layer], post_ln_b[layer], mix_tm, f"mix_{nm}")
            new_a.append(buf)
        else:
            p3 = _proj(x.reshape(b * t, d), w_in_odd[i], proj_tm, PROJ_COL_TILE,
                       f"proj_{nm}").reshape(b, t, ODD_IN)
            x, buf = _odd_mix(x, p3, conv_c_bufs[i], conv_c_w[i], conv_c_b[i], ln_c_g[i], ln_c_b[i],
                              w_out_odd[i], post_ln_g[layer], post_ln_b[layer], mix_tm, f"mix_{nm}")
            new_c.append(buf)
    if prompt:
        k_all, v_all = (jnp.swapaxes(kv, 3, 4) for kv in stacked_kv)
    else:
        k_all, v_all = jnp.stack(new_k), jnp.stack(new_v)
    return x, k_all, v_all, jnp.stack(new_a), jnp.stack(new_c)


def kernel(x_prompt, x_sample, cache_sb_k, cache_sb_v, state_conv_a, state_conv_c, meta_tokens,
           w_in_even, conv_a_w, w_out_even, w_in_odd, conv_c_w, conv_c_b, ln_c_g, ln_c_b,
           w_out_odd, post_ln_g, post_ln_b):
    weights = (w_in_even.astype(BF16), conv_a_w, w_out_even.astype(BF16), w_in_odd.astype(BF16),
               conv_c_w, conv_c_b, ln_c_g, ln_c_b, w_out_odd.astype(BF16), post_ln_g, post_ln_b)
    b_p = x_prompt.shape[0]
    n_even = w_in_even.shape[0]
    n_odd = w_in_odd.shape[0]
    meta = jnp.broadcast_to(meta_tokens.astype(x_prompt.dtype)[None], (b_p, N_META, D_MODEL))
    xp = jnp.concatenate([meta, x_prompt], axis=1)
    zeros_a = jnp.zeros((n_even, b_p, A_CONV_W - 1, A_WIDTH), x_prompt.dtype)
    zeros_c = jnp.zeros((n_odd, b_p, C_CONV_W - 1, C_WIDTH), x_prompt.dtype)
    h_p, k_p, v_p, a_p, c_p = _trunk(xp, zeros_a, zeros_c, None, None, weights, "p")
    y_sample, k_s, v_s, a_s, c_s = _trunk(x_sample, state_conv_a, state_conv_c, cache_sb_k,
                                          cache_sb_v, weights, "s")
    return (h_p[:, N_META:], y_sample, k_p, v_p, a_p, c_p, k_s, v_s, a_s, c_s)
```

```python
import functools
import math

import jax
import jax.numpy as jnp
from jax import lax
from jax.experimental import pallas as pl
from jax.experimental.pallas import tpu as pltpu

F32 = jnp.float32
BF16 = jnp.bfloat16

D_MODEL = 1024
DEPTH = 4
N_META = 16
HEADS = 16
HEAD_DIM = 64
A_WIDTH = D_MODEL
A_CONV_W = 3
C_WIDTH = 2 * D_MODEL
C_CONV_W = 31
EVEN_IN = 8 * D_MODEL
ODD_IN = 3 * C_WIDTH
ALPHA = (2 * DEPTH) ** 0.25
LN_EPS = 1e-5
Q_SCALE = 1.0 / math.sqrt(HEAD_DIM)

LANES = 128
KEY_BLOCK = 128
VMEM_LIMIT = 56 * 1024 * 1024


def _params(n_axes):
    return pltpu.CompilerParams(dimension_semantics=("arbitrary",) * n_axes,
                                vmem_limit_bytes=VMEM_LIMIT)


def _proj_kernel(x_ref, w_ref, o_ref, xb_ref):
    @pl.when(pl.program_id(1) == 0)
    def _():
        xb_ref[...] = x_ref[...].astype(BF16)

    o_ref[...] = jnp.dot(xb_ref[...], w_ref[...], preferred_element_type=F32)


def _proj(x2d, w_bf16, tm, tn, name):
    m, k = x2d.shape
    n = w_bf16.shape[1]
    return pl.pallas_call(
        _proj_kernel,
        grid=(m // tm, n // tn),
        in_specs=[pl.BlockSpec((tm, k), lambda i, j: (i, 0)),
                  pl.BlockSpec((k, tn), lambda i, j: (0, j))],
        out_specs=pl.BlockSpec((tm, tn), lambda i, j: (i, j)),
        out_shape=jax.ShapeDtypeStruct((m, n), F32),
        scratch_shapes=[pltpu.VMEM((tm, k), BF16)],
        compiler_params=_params(2),
        name=name,
    )(x2d, w_bf16)


def _suffix_sum_matrix():
    r = lax.broadcasted_iota(jnp.int32, (2 * KEY_BLOCK, 2 * KEY_BLOCK), 0) & (KEY_BLOCK - 1)
    c = lax.broadcasted_iota(jnp.int32, (2 * KEY_BLOCK, 2 * KEY_BLOCK), 1)
    return jnp.where((c >= KEY_BLOCK) | (r > c), 1.0, 0.0).astype(BF16)


CHUNK = 2 * KEY_BLOCK
MASKED_LOGIT = -1e30


def _chunk_suffix_matrix():
    r = lax.broadcasted_iota(jnp.int32, (CHUNK, CHUNK + KEY_BLOCK), 0)
    c = lax.broadcasted_iota(jnp.int32, (CHUNK, CHUNK + KEY_BLOCK), 1)
    return jnp.where((c >= CHUNK) | (r > c), 1.0, 0.0).astype(BF16)


def _sb_scores(q, kt, msuf, mask):
    z = jnp.dot(q, kt, preferred_element_type=F32)
    lse = jnp.log(1.0 + jnp.exp(-jnp.abs(z)))
    log_beta = jnp.minimum(z, 0.0) - lse
    log_1m = log_beta - z
    if mask is not None:
        log_1m = jnp.where(mask, log_1m, 0.0)
    c = jnp.dot(log_1m.astype(BF16), msuf, preferred_element_type=F32)
    s = log_beta + c[:, :CHUNK]
    if mask is not None:
        s = jnp.where(mask, s, MASKED_LOGIT)
    return s, c[:, CHUNK:]


def _sb_weights(s, later, rows):
    if later is not None:
        s = s + jnp.concatenate([later, later], axis=1)
    a = jnp.exp(s).astype(BF16)
    return jnp.concatenate([a[:rows], a[rows:]], axis=1)


def _attn_prompt_kernel(q_ref, k_ref, v_ref, *rest, seq, first_layer):
    o_ref, nk_ref, nv_ref, qs, kts, vs = rest[-6:]
    chunk = CHUNK
    n_full = seq // chunk
    tail = seq - n_full * chunk
    padded = (n_full + 1) * chunk
    msuf = _chunk_suffix_matrix()
    lane = lax.broadcasted_iota(jnp.int32, (seq, LANES), 1)
    first = lane < HEAD_DIM

    k = k_ref[0]
    v = v_ref[0]
    q = q_ref[0] * Q_SCALE
    if first_layer:
        for later_slot in range(1, nk_ref.shape[0]):
            nk_ref[later_slot] = jnp.zeros(nk_ref.shape[1:], F32)
            nv_ref[later_slot] = jnp.zeros(nv_ref.shape[1:], F32)
    qs[0] = jnp.where(first, q, 0.0).astype(BF16)
    qs[1] = jnp.where(first, 0.0, q).astype(BF16)
    vs[0, 0:seq, :] = jnp.where(first, v, 0.0).astype(BF16)
    vs[1, 0:seq, :] = jnp.where(first, 0.0, v).astype(BF16)
    zpad = jnp.zeros((padded - seq, LANES), BF16)
    vs[0, seq:padded, :] = zpad
    vs[1, seq:padded, :] = zpad
    zrows = jnp.zeros((chunk - tail, LANES), F32)
    for j in range(n_full + 1):
        width = chunk if j < n_full else tail
        cols = slice(j * chunk, j * chunk + width)
        if j < n_full:
            kt = k[cols, :].T
            vt = v[cols, :].T
        else:
            kt = jnp.concatenate([k[cols, :], zrows], axis=0).T
            vt = jnp.concatenate([v[cols, :], zrows], axis=0).T
        kts[:, j * chunk:(j + 1) * chunk] = kt.astype(BF16)
        for h in range(2):
            rows_h = slice(h * HEAD_DIM, (h + 1) * HEAD_DIM)
            nk_ref[0, 0, h, :, cols] = kt[rows_h, :width]
            nv_ref[0, 0, h, :, cols] = vt[rows_h, :width]

    def q_tile(base, seg_rows, n_before):
        def weighted_values(a2, kb0, acc):
            vv = jnp.concatenate([vs[0, pl.ds(kb0, chunk), :], vs[1, pl.ds(kb0, chunk), :]], axis=0)
            return acc + jnp.dot(a2, vv, preferred_element_type=F32)

        q_heads = [[], []]
        later = acc = a2 = None
        all_rows = 0
        for k in reversed(range(len(seg_rows))):
            start = base + k * chunk
            if not isinstance(start, int):
                start = pl.multiple_of(start, chunk)
            rk = seg_rows[k]
            q_heads = [[qs[h, pl.ds(start, rk), :]] + q_heads[h] for h in range(2)]
            joined = all_rows
            all_rows = rk + joined
            if a2 is not None:
                acc = weighted_values(a2, start + chunk, acc)
                gap = jnp.zeros((rk, LANES), F32)
                later = jnp.concatenate([gap, later[:joined], gap, later[joined:]], axis=0)
                acc = jnp.concatenate([gap, acc], axis=0)
            else:
                acc = jnp.zeros((rk, LANES), F32)
            row = lax.broadcasted_iota(jnp.int32, (all_rows, chunk), 0)
            col = lax.broadcasted_iota(jnp.int32, (all_rows, chunk), 1)
            own = (col < row) | (row >= rk)
            qq = jnp.concatenate(q_heads[0] + q_heads[1], axis=0)
            s, tot = _sb_scores(qq, kts[:, pl.ds(start, chunk)], msuf, jnp.concatenate([own, own], axis=0))
            a2 = _sb_weights(s, later, all_rows)
            later = tot if later is None else later + tot
        acc0 = acc

        def body(i, carry):
            a2, later, acc = carry
            kb0 = pl.multiple_of((n_before - 1 - i) * chunk, chunk)
            acc = weighted_values(a2, kb0 + chunk, acc)
            s, tot = _sb_scores(qq, kts[:, pl.ds(kb0, chunk)], msuf, None)
            return _sb_weights(s, later, all_rows), later + tot, acc

        a2, _, acc = lax.fori_loop(0, n_before, body, (a2, later, acc0))
        out = weighted_values(a2, 0, acc)
        offset = 0
        for k, rk in enumerate(seg_rows):
            o_ref[0, pl.ds(base + k * chunk, rk), :] = out[offset:offset + rk]
            offset += rk

    assert n_full >= 2 and n_full % 2 == 0
    n_tiles = n_full // 2

    def full_tile(m, carry):
        q_tile(pl.multiple_of(2 * m * chunk, 2 * chunk), (chunk, chunk), 2 * m)
        return carry

    lax.fori_loop(0, n_tiles - 1 if tail else n_tiles, full_tile, 0)
    if tail:
        q_tile((n_full - 2) * chunk, (chunk, chunk, tail), n_full - 2)


def _attn_prompt(p3, slot, n_slots, stacked_kv, name):
    b, t, _ = p3.shape
    padded = (t // (2 * KEY_BLOCK) + 1) * 2 * KEY_BLOCK
    col0 = 4 * A_WIDTH // LANES
    per = D_MODEL // LANES
    first = stacked_kv is None
    assert first == (slot == 0)

    def col_spec(which):
        return pl.BlockSpec((1, t, LANES), lambda i, j: (i, 0, col0 + which * per + j))

    kv_shape = jax.ShapeDtypeStruct((n_slots, b, HEADS, HEAD_DIM, t), F32)
    if first:
        kv_spec = pl.BlockSpec((n_slots, 1, 2, HEAD_DIM, t), lambda i, j: (0, i, j, 0, 0))
        extra_specs, extra_args, aliases = [], (), {}
    else:
        kv_spec = pl.BlockSpec((1, 1, 2, HEAD_DIM, t), lambda i, j: (slot, i, j, 0, 0))
        extra_specs = [pl.BlockSpec(memory_space=pl.ANY)] * 2
        extra_args = tuple(stacked_kv)
        aliases = {3: 1, 4: 2}
    return pl.pallas_call(
        functools.partial(_attn_prompt_kernel, seq=t, first_layer=first),
        grid=(b, HEADS // 2),
        in_specs=[col_spec(0), col_spec(1), col_spec(2)] + extra_specs,
        out_specs=[pl.BlockSpec((1, t, LANES), lambda i, j: (i, 0, j)), kv_spec, kv_spec],
        out_shape=[jax.ShapeDtypeStruct((b, t, D_MODEL), F32), kv_shape, kv_shape],
        scratch_shapes=[pltpu.VMEM((2, t, LANES), BF16),
                        pltpu.VMEM((LANES, padded), BF16),
                        pltpu.VMEM((2, padded, LANES), BF16)],
        input_output_aliases=aliases,
        compiler_params=_params(2),
        name=name,
    )(p3, p3, p3, *extra_args)


def _attn_sample_kernel(q_ref, k_ref, v_ref, ckt_ref, cvt_ref, ckl_ref, cvl_ref, o_ref, nk_ref, nv_ref,
                        *, past, new):
    n_blocks = (past + new + KEY_BLOCK - 1) // KEY_BLOCK
    total = n_blocks * KEY_BLOCK
    last = (n_blocks - 1) * KEY_BLOCK
    m2 = _suffix_sum_matrix()
    row = lax.broadcasted_iota(jnp.int32, (new, KEY_BLOCK), 0)
    col = lax.broadcasted_iota(jnp.int32, (new, KEY_BLOCK), 1)
    mask = col + last < row + past
    zero_rows = jnp.zeros((total - past - new, HEAD_DIM), F32)
    nt_dims = (((1,), (1,)), ((), ()))

    q2 = q_ref[0] * Q_SCALE
    k2 = k_ref[0]
    v2 = v_ref[0]
    outs = []
    for h in range(2):
        lanes = slice(h * HEAD_DIM, (h + 1) * HEAD_DIM)
        q = q2[:, lanes].astype(BF16)
        kn = k2[:, lanes]
        vn = v2[:, lanes]
        nk_ref[0, h] = kn
        nv_ref[0, h] = vn
        k_last = jnp.concatenate([ckl_ref[0, 0, h], kn, zero_rows], axis=0).astype(BF16)
        v_last = jnp.concatenate([cvl_ref[0, 0, h], vn, zero_rows], axis=0).astype(BF16)
        z = jnp.concatenate(
            [jnp.dot(q, ckt_ref[0, 0, h].astype(BF16), preferred_element_type=F32),
             lax.dot_general(q, k_last, nt_dims, preferred_element_type=F32)], axis=1)
        lse = jnp.log(1.0 + jnp.exp(-jnp.abs(z)))
        log_beta = jnp.minimum(z, 0.0) - lse
        log_1m = log_beta - z
        blocks = [log_1m[:, j * KEY_BLOCK:(j + 1) * KEY_BLOCK] for j in range(n_blocks)]
        blocks[-1] = jnp.where(mask, blocks[-1], 0.0)
        stacked = jnp.concatenate(blocks, axis=0)
        hi = stacked.astype(BF16)
        lo = (stacked - hi.astype(F32)).astype(BF16)
        c = jnp.dot(jnp.concatenate([hi, lo], axis=1), m2, preferred_element_type=F32)
        later = jnp.zeros((new, KEY_BLOCK), F32)
        between = [None] * n_blocks
        for j in reversed(range(n_blocks)):
            cj = c[j * new:(j + 1) * new]
            between[j] = cj[:, :KEY_BLOCK] + later
            later = later + cj[:, KEY_BLOCK:]
        a = jnp.exp(log_beta + jnp.concatenate(between, axis=1))
        a_last = jnp.where(mask, a[:, last:], 0.0).astype(BF16)
        out = lax.dot_general(a[:, :last].astype(BF16), cvt_ref[0, 0, h].astype(BF16), nt_dims,
                              preferred_element_type=F32)
        outs.append(out + jnp.dot(a_last, v_last, preferred_element_type=F32))
    o_ref[0] = jnp.concatenate(outs, axis=1)


def _attn_sample(p3, cache_t, cache_last, layer, name):
    b, new, _ = p3.shape
    past = cache_t[0].shape[4]
    last = past // KEY_BLOCK * KEY_BLOCK
    assert 0 < past - last and past - last + new <= KEY_BLOCK
    col0 = 4 * A_WIDTH // LANES
    per = D_MODEL // LANES

    def col_spec(which):
        return pl.BlockSpec((1, new, LANES), lambda i, j: (i, 0, col0 + which * per + j))

    cache_spec = pl.BlockSpec((1, 1, 2, HEAD_DIM, last), lambda i, j: (layer, i, j, 0, 0))
    last_spec = pl.BlockSpec((1, 1, 2, past - last, HEAD_DIM), lambda i, j: (layer, i, j, 0, 0))
    kv_shape = jax.ShapeDtypeStruct((b, HEADS, new, HEAD_DIM), F32)
    kv_spec = pl.BlockSpec((1, 2, new, HEAD_DIM), lambda i, j: (i, j, 0, 0))
    return pl.pallas_call(
        functools.partial(_attn_sample_kernel, past=past, new=new),
        grid=(b, HEADS // 2),
        in_specs=[col_spec(0), col_spec(1), col_spec(2), cache_spec, cache_spec, last_spec, last_spec],
        out_specs=[pl.BlockSpec((1, new, LANES), lambda i, j: (i, 0, j)), kv_spec, kv_spec],
        out_shape=[jax.ShapeDtypeStruct((b, new, D_MODEL), F32), kv_shape, kv_shape],
        compiler_params=_params(2),
        name=name,
    )(p3, p3, p3, *cache_t, *cache_last)


def _sigmoid(x):
    return 0.5 * jnp.tanh(0.5 * x) + 0.5


def _silu(x):
    return x * _sigmoid(x)


def _layer_norm(x, g, b):
    mu = jnp.mean(x, axis=-1, keepdims=True)
    d = x - mu
    var = jnp.mean(d * d, axis=-1, keepdims=True)
    return d * lax.rsqrt(var + LN_EPS) * g + b


def _even_mix_kernel(x_ref, h_ref, gb_ref, gc_ref, za_ref, zb_ref, o_ref, buf_ref, cw_ref, wo_ref,
                     g_ref, b_ref, y_ref, nbuf_ref, ext, *, tm):
    hist = A_CONV_W - 1
    pad = 8

    @pl.when(pl.program_id(1) == 0)
    def _():
        ext[pad - hist:pad, :] = buf_ref[0]

    @pl.when(pl.program_id(1) > 0)
    def _():
        ext[0:pad, :] = ext[tm:tm + pad, :]

    u = gc_ref[0] * h_ref[0]
    ext[pad:pad + tm, :] = u
    cw = cw_ref[...]
    conv = cw[2:3, :] * u
    for tap in range(hist):
        conv = conv + cw[tap:tap + 1, :] * ext[pad - hist + tap:pad - hist + tap + tm, :]
    nbuf_ref[0] = ext[tm + pad - hist:tm + pad, :]
    y_a = gb_ref[0] * conv * _silu(za_ref[0])
    y_b = o_ref[0] * _silu(zb_ref[0])
    out = jnp.dot(y_a.astype(BF16), wo_ref[0:A_WIDTH, :], preferred_element_type=F32)
    out = out + jnp.dot(y_b.astype(BF16), wo_ref[A_WIDTH:, :], preferred_element_type=F32)
    y_ref[0] = _layer_norm(ALPHA * x_ref[0] + out, g_ref[...], b_ref[...])


def _even_mix(x, p3, o, conv_buf, conv_w, w_out_bf16, ln_g, ln_b, tm, name):
    b, t, d = x.shape
    hist = A_CONV_W - 1

    def row_spec(col):
        return pl.BlockSpec((1, tm, d), lambda i, j: (i, j, col))

    def full_spec(shape):
        return pl.BlockSpec(shape, lambda i, j: (0,) * len(shape))

    buf_spec = pl.BlockSpec((1, hist, A_WIDTH), lambda i, j: (i, 0, 0))
    return pl.pallas_call(
        functools.partial(_even_mix_kernel, tm=tm),
        grid=(b, t // tm),
        in_specs=[row_spec(0), row_spec(0), row_spec(1), row_spec(2), row_spec(3), row_spec(7),
                  row_spec(0), buf_spec, full_spec((A_CONV_W, A_WIDTH)),
                  full_spec((2 * D_MODEL, D_MODEL)), full_spec((1, d)), full_spec((1, d))],
        out_specs=[row_spec(0), buf_spec],
        out_shape=[jax.ShapeDtypeStruct((b, t, d), F32),
                   jax.ShapeDtypeStruct((b, hist, A_WIDTH), F32)],
        scratch_shapes=[pltpu.VMEM((tm + 8, A_WIDTH), F32)],
        compiler_params=_params(2),
        name=name,
    )(x, p3, p3, p3, p3, p3, o, conv_buf, conv_w, w_out_bf16, ln_g[None], ln_b[None])


def _odd_mix_kernel(x_ref, a_ref, gate_ref, zc_ref, buf_ref, cw_ref, cb_ref, cg_ref, cbeta_ref, wo_ref,
                    g_ref, b_ref, y_ref, nbuf_ref, ext, conv_s, shifted, *, tm):
    hist = C_CONV_W - 1
    pad = 32

    @pl.when(pl.program_id(1) == 0)
    def _():
        ext[pad - hist:pad, :] = buf_ref[0]

    @pl.when(pl.program_id(1) > 0)
    def _():
        ext[0:pad, :] = ext[tm:tm + pad, :]

    ext[pad:pad + tm, :] = a_ref[0] * _sigmoid(gate_ref[0])
    nbuf_ref[0] = ext[tm + pad - hist:tm + pad, :]

    sub = 8

    def lane_chunk(c, carry):
        lanes = pl.ds(pl.multiple_of(c * LANES, LANES), LANES)
        for phase in range(sub):
            span = tm + pad if phase == 0 else tm + pad - sub
            shifted[phase, 0:span, :] = ext[phase:phase + span, lanes]
        acc = jnp.zeros((tm, LANES), F32) + cb_ref[:, lanes]
        for tap in range(C_CONV_W):
            first_row = pad - hist + tap
            phase = first_row % sub
            acc = acc + cw_ref[tap:tap + 1, lanes] * shifted[phase, first_row - phase:first_row - phase + tm, :]
        conv_s[:, lanes] = acc
        return carry

    lax.fori_loop(0, C_WIDTH // LANES, lane_chunk, 0)
    y = _silu(_layer_norm(conv_s[...], cg_ref[...], cbeta_ref[...])) * _silu(zc_ref[0])
    out = jnp.dot(y.astype(BF16), wo_ref[...], preferred_element_type=F32)
    y_ref[0] = _layer_norm(ALPHA * x_ref[0] + out, g_ref[...], b_ref[...])


def _odd_mix(x, p3, conv_buf, conv_w, conv_b, ln_c_g, ln_c_b, w_out_bf16, ln_g, ln_b, tm, name):
    b, t, d = x.shape
    hist = C_CONV_W - 1

    def row_spec(width, col):
        return pl.BlockSpec((1, tm, width), lambda i, j: (i, j, col))

    def full_spec(shape):
        return pl.BlockSpec(shape, lambda i, j: (0,) * len(shape))

    buf_spec = pl.BlockSpec((1, hist, C_WIDTH), lambda i, j: (i, 0, 0))
    return pl.pallas_call(
        functools.partial(_odd_mix_kernel, tm=tm),
        grid=(b, t // tm),
        in_specs=[row_spec(d, 0), row_spec(C_WIDTH, 0), row_spec(C_WIDTH, 1), row_spec(C_WIDTH, 2),
                  buf_spec, full_spec((C_CONV_W, C_WIDTH)), full_spec((1, C_WIDTH)),
                  full_spec((1, C_WIDTH)), full_spec((1, C_WIDTH)), full_spec((C_WIDTH, D_MODEL)),
                  full_spec((1, d)), full_spec((1, d))],
        out_specs=[row_spec(d, 0), buf_spec],
        out_shape=[jax.ShapeDtypeStruct((b, t, d), F32),
                   jax.ShapeDtypeStruct((b, hist, C_WIDTH), F32)],
        scratch_shapes=[pltpu.VMEM((tm + 32, C_WIDTH), F32), pltpu.VMEM((tm, C_WIDTH), F32),
                        pltpu.VMEM((8, tm + 32, LANES), F32)],
        compiler_params=_params(2),
        name=name,
    )(x, p3, p3, p3, conv_buf, conv_w, conv_b[None], ln_c_g[None], ln_c_b[None], w_out_bf16,
      ln_g[None], ln_b[None])


PROMPT_ROW_TILE = 344
PROJ_COL_TILE = 1024


def _trunk(x, conv_a_bufs, conv_c_bufs, cache_k, cache_v, weights, tag):
    (w_in_even, conv_a_w, w_out_even, w_in_odd, conv_c_w, conv_c_b, ln_c_g, ln_c_b, w_out_odd,
     post_ln_g, post_ln_b) = weights
    b, t, d = x.shape
    prompt = cache_k is None
    if not prompt:
        whole = cache_k.shape[3] // KEY_BLOCK * KEY_BLOCK
        cache_t = (jnp.swapaxes(cache_k, 3, 4), jnp.swapaxes(cache_v, 3, 4))
        cache_last = (cache_k[:, :, :, whole:, :], cache_v[:, :, :, whole:, :])
    proj_tm = t if prompt else b * t
    mix_tm = PROMPT_ROW_TILE if prompt else t
    n_even = w_in_even.shape[0]
    new_k, new_v, new_a, new_c = [], [], [], []
    stacked_kv = None
    for layer in range(DEPTH):
        i = layer // 2
        nm = f"{tag}{layer}"
        if layer % 2 == 0:
            p3 = _proj(x.reshape(b * t, d), w_in_even[i], proj_tm, PROJ_COL_TILE,
                       f"proj_{nm}").reshape(b, t, EVEN_IN)
            if prompt:
                o, *stacked_kv = _attn_prompt(p3, i, n_even, stacked_kv, f"attn_{nm}")
            else:
                o, kh, vh = _attn_sample(p3, cache_t, cache_last, i, f"attn_{nm}")
                new_k.append(kh)
                new_v.append(vh)
            x, buf = _even_mix(x, p3, o, conv_a_bufs[i], conv_a_w[i], w_out_even[i],
                               post_ln_g[layer], post_ln_b[layer], mix_tm, f"mix_{nm}")
            new_a.append(buf)
        else:
            p3 = _proj(x.reshape(b * t, d), w_in_odd[i], proj_tm, PROJ_COL_TILE,
                       f"proj_{nm}").reshape(b, t, ODD_IN)
            x, buf = _odd_mix(x, p3, conv_c_bufs[i], conv_c_w[i], conv_c_b[i], ln_c_g[i], ln_c_b[i],
                              w_out_odd[i], post_ln_g[layer], post_ln_b[layer], mix_tm, f"mix_{nm}")
            new_c.append(buf)
    if prompt:
        k_all, v_all = (jnp.swapaxes(kv, 3, 4) for kv in stacked_kv)
    else:
        k_all, v_all = jnp.stack(new_k), jnp.stack(new_v)
    return x, k_all, v_all, jnp.stack(new_a), jnp.stack(new_c)


def kernel(x_prompt, x_sample, cache_sb_k, cache_sb_v, state_conv_a, state_conv_c, meta_tokens,
           w_in_even, conv_a_w, w_out_even, w_in_odd, conv_c_w, conv_c_b, ln_c_g, ln_c_b,
           w_out_odd, post_ln_g, post_ln_b):
    weights = (w_in_even.astype(BF16), conv_a_w, w_out_even.astype(BF16), w_in_odd.astype(BF16),
               conv_c_w, conv_c_b, ln_c_g, ln_c_b, w_out_odd.astype(BF16), post_ln_g, post_ln_b)
    b_p = x_prompt.shape[0]
    n_even = w_in_even.shape[0]
    n_odd = w_in_odd.shape[0]
    meta = jnp.broadcast_to(meta_tokens.astype(x_prompt.dtype)[None], (b_p, N_META, D_MODEL))
    xp = jnp.concatenate([meta, x_prompt], axis=1)
    zeros_a = jnp.zeros((n_even, b_p, A_CONV_W - 1, A_WIDTH), x_prompt.dtype)
    zeros_c = jnp.zeros((n_odd, b_p, C_CONV_W - 1, C_WIDTH), x_prompt.dtype)
    h_p, k_p, v_p, a_p, c_p = _trunk(xp, zeros_a, zeros_c, None, None, weights, "p")
    y_sample, k_s, v_s, a_s, c_s = _trunk(x_sample, state_conv_a, state_conv_c, cache_sb_k,
                                          cache_sb_v, weights, "s")
    return (h_p[:, N_META:], y_sample, k_p, v_p, a_p, c_p, k_s, v_s, a_s, c_s)
```

```python
import functools
import math

import jax
import jax.numpy as jnp
from jax import lax
from jax.experimental import pallas as pl
from jax.experimental.pallas import tpu as pltpu

F32 = jnp.float32
BF16 = jnp.bfloat16

D_MODEL = 1024
DEPTH = 4
N_META = 16
HEADS = 16
HEAD_DIM = 64
A_WIDTH = D_MODEL
A_CONV_W = 3
C_WIDTH = 2 * D_MODEL
C_CONV_W = 31
EVEN_IN = 8 * D_MODEL
ODD_IN = 3 * C_WIDTH
ALPHA = (2 * DEPTH) ** 0.25
LN_EPS = 1e-5
Q_SCALE = 1.0 / math.sqrt(HEAD_DIM)

LANES = 128
KEY_BLOCK = 128
VMEM_LIMIT = 56 * 1024 * 1024


def _params(n_axes):
    return pltpu.CompilerParams(dimension_semantics=("arbitrary",) * n_axes,
                                vmem_limit_bytes=VMEM_LIMIT)


def _proj_kernel(x_ref, w_ref, o_ref, xb_ref):
    @pl.when(pl.program_id(1) == 0)
    def _():
        xb_ref[...] = x_ref[...].astype(BF16)

    o_ref[...] = jnp.dot(xb_ref[...], w_ref[...], preferred_element_type=F32)


def _proj(x2d, w_bf16, tm, tn, name):
    m, k = x2d.shape
    n = w_bf16.shape[1]
    return pl.pallas_call(
        _proj_kernel,
        grid=(m // tm, n // tn),
        in_specs=[pl.BlockSpec((tm, k), lambda i, j: (i, 0)),
                  pl.BlockSpec((k, tn), lambda i, j: (0, j))],
        out_specs=pl.BlockSpec((tm, tn), lambda i, j: (i, j)),
        out_shape=jax.ShapeDtypeStruct((m, n), F32),
        scratch_shapes=[pltpu.VMEM((tm, k), BF16)],
        compiler_params=_params(2),
        name=name,
    )(x2d, w_bf16)


def _suffix_sum_matrix():
    r = lax.broadcasted_iota(jnp.int32, (2 * KEY_BLOCK, 2 * KEY_BLOCK), 0) & (KEY_BLOCK - 1)
    c = lax.broadcasted_iota(jnp.int32, (2 * KEY_BLOCK, 2 * KEY_BLOCK), 1)
    return jnp.where((c >= KEY_BLOCK) | (r > c), 1.0, 0.0).astype(BF16)


CHUNK = 2 * KEY_BLOCK
TILE_CHUNKS = 4
MASKED_LOGIT = -1e30


def _chunk_suffix_matrix():
    r = lax.broadcasted_iota(jnp.int32, (CHUNK, CHUNK + KEY_BLOCK), 0)
    c = lax.broadcasted_iota(jnp.int32, (CHUNK, CHUNK + KEY_BLOCK), 1)
    return jnp.where((c >= CHUNK) | (r > c), 1.0, 0.0).astype(BF16)


def _sb_scores(q, kt, msuf, mask):
    z = jnp.dot(q, kt, preferred_element_type=F32)
    lse = jnp.log(1.0 + jnp.exp(-jnp.abs(z)))
    log_beta = jnp.minimum(z, 0.0) - lse
    log_1m = log_beta - z
    if mask is not None:
        log_1m = jnp.where(mask, log_1m, 0.0)
    c = jnp.dot(log_1m.astype(BF16), msuf, preferred_element_type=F32)
    s = log_beta + c[:, :CHUNK]
    if mask is not None:
        s = jnp.where(mask, s, MASKED_LOGIT)
    return s, c[:, CHUNK:]


def _sb_weights(s, later, rows):
    if later is not None:
        s = s + jnp.concatenate([later, later], axis=1)
    a = jnp.exp(s).astype(BF16)
    return jnp.concatenate([a[:rows], a[rows:]], axis=1)


def _attn_prompt_kernel(q_ref, k_ref, v_ref, *rest, seq, first_layer):
    o_ref, nk_ref, nv_ref, qs, kts, vs = rest[-6:]
    chunk = CHUNK
    n_full = seq // chunk
    tail = seq - n_full * chunk
    padded = (n_full + 1) * chunk
    msuf = _chunk_suffix_matrix()
    lane = lax.broadcasted_iota(jnp.int32, (seq, LANES), 1)
    first = lane < HEAD_DIM

    k = k_ref[0]
    v = v_ref[0]
    q = q_ref[0] * Q_SCALE
    if first_layer:
        for later_slot in range(1, nk_ref.shape[0]):
            nk_ref[later_slot] = jnp.zeros(nk_ref.shape[1:], F32)
            nv_ref[later_slot] = jnp.zeros(nv_ref.shape[1:], F32)
    qs[0] = jnp.where(first, q, 0.0).astype(BF16)
    qs[1] = jnp.where(first, 0.0, q).astype(BF16)
    vs[0, 0:seq, :] = jnp.where(first, v, 0.0).astype(BF16)
    vs[1, 0:seq, :] = jnp.where(first, 0.0, v).astype(BF16)
    zpad = jnp.zeros((padded - seq, LANES), BF16)
    vs[0, seq:padded, :] = zpad
    vs[1, seq:padded, :] = zpad
    zrows = jnp.zeros((chunk - tail, LANES), F32)
    for j in range(n_full + 1):
        width = chunk if j < n_full else tail
        cols = slice(j * chunk, j * chunk + width)
        if j < n_full:
            kt = k[cols, :].T
            vt = v[cols, :].T
        else:
            kt = jnp.concatenate([k[cols, :], zrows], axis=0).T
            vt = jnp.concatenate([v[cols, :], zrows], axis=0).T
        kts[:, j * chunk:(j + 1) * chunk] = kt.astype(BF16)
        for h in range(2):
            rows_h = slice(h * HEAD_DIM, (h + 1) * HEAD_DIM)
            nk_ref[0, 0, h, :, cols] = kt[rows_h, :width]
            nv_ref[0, 0, h, :, cols] = vt[rows_h, :width]

    def q_tile(base, seg_rows, n_before):
        def weighted_values(a2, kb0, acc):
            vv = jnp.concatenate([vs[0, pl.ds(kb0, chunk), :], vs[1, pl.ds(kb0, chunk), :]], axis=0)
            return acc + jnp.dot(a2, vv, preferred_element_type=F32)

        q_heads = [[], []]
        later = acc = a2 = None
        all_rows = 0
        for k in reversed(range(len(seg_rows))):
            start = base + k * chunk
            if not isinstance(start, int):
                start = pl.multiple_of(start, chunk)
            rk = seg_rows[k]
            q_heads = [[qs[h, pl.ds(start, rk), :]] + q_heads[h] for h in range(2)]
            joined = all_rows
            all_rows = rk + joined
            if a2 is not None:
                acc = weighted_values(a2, start + chunk, acc)
                gap = jnp.zeros((rk, LANES), F32)
                later = jnp.concatenate([gap, later[:joined], gap, later[joined:]], axis=0)
                acc = jnp.concatenate([gap, acc], axis=0)
            else:
                acc = jnp.zeros((rk, LANES), F32)
            row = lax.broadcasted_iota(jnp.int32, (all_rows, chunk), 0)
            col = lax.broadcasted_iota(jnp.int32, (all_rows, chunk), 1)
            own = (col < row) | (row >= rk)
            qq = jnp.concatenate(q_heads[0] + q_heads[1], axis=0)
            s, tot = _sb_scores(qq, kts[:, pl.ds(start, chunk)], msuf, jnp.concatenate([own, own], axis=0))
            a2 = _sb_weights(s, later, all_rows)
            later = tot if later is None else later + tot
        acc0 = acc

        def body(i, carry):
            a2, later, acc = carry
            kb0 = pl.multiple_of((n_before - 1 - i) * chunk, chunk)
            acc = weighted_values(a2, kb0 + chunk, acc)
            s, tot = _sb_scores(qq, kts[:, pl.ds(kb0, chunk)], msuf, None)
            return _sb_weights(s, later, all_rows), later + tot, acc

        a2, _, acc = lax.fori_loop(0, n_before, body, (a2, later, acc0))
        out = weighted_values(a2, 0, acc)
        offset = 0
        for k, rk in enumerate(seg_rows):
            o_ref[0, pl.ds(base + k * chunk, rk), :] = out[offset:offset + rk]
            offset += rk

    per_tile = TILE_CHUNKS
    assert n_full >= per_tile and n_full % per_tile == 0
    n_tiles = n_full // per_tile

    def full_tile(m, carry):
        q_tile(pl.multiple_of(per_tile * m * chunk, per_tile * chunk), (chunk,) * per_tile, per_tile * m)
        return carry

    lax.fori_loop(0, n_tiles - 1 if tail else n_tiles, full_tile, 0)
    if tail:
        q_tile((n_full - per_tile) * chunk, (chunk,) * per_tile + (tail,), n_full - per_tile)


def _attn_prompt(p3, slot, n_slots, stacked_kv, name):
    b, t, _ = p3.shape
    padded = (t // (2 * KEY_BLOCK) + 1) * 2 * KEY_BLOCK
    col0 = 4 * A_WIDTH // LANES
    per = D_MODEL // LANES
    first = stacked_kv is None
    assert first == (slot == 0)

    def col_spec(which):
        return pl.BlockSpec((1, t, LANES), lambda i, j: (i, 0, col0 + which * per + j))

    kv_shape = jax.ShapeDtypeStruct((n_slots, b, HEADS, HEAD_DIM, t), F32)
    if first:
        kv_spec = pl.BlockSpec((n_slots, 1, 2, HEAD_DIM, t), lambda i, j: (0, i, j, 0, 0))
        extra_specs, extra_args, aliases = [], (), {}
    else:
        kv_spec = pl.BlockSpec((1, 1, 2, HEAD_DIM, t), lambda i, j: (slot, i, j, 0, 0))
        extra_specs = [pl.BlockSpec(memory_space=pl.ANY)] * 2
        extra_args = tuple(stacked_kv)
        aliases = {3: 1, 4: 2}
    return pl.pallas_call(
        functools.partial(_attn_prompt_kernel, seq=t, first_layer=first),
        grid=(b, HEADS // 2),
        in_specs=[col_spec(0), col_spec(1), col_spec(2)] + extra_specs,
        out_specs=[pl.BlockSpec((1, t, LANES), lambda i, j: (i, 0, j)), kv_spec, kv_spec],
        out_shape=[jax.ShapeDtypeStruct((b, t, D_MODEL), F32), kv_shape, kv_shape],
        scratch_shapes=[pltpu.VMEM((2, t, LANES), BF16),
                        pltpu.VMEM((LANES, padded), BF16),
                        pltpu.VMEM((2, padded, LANES), BF16)],
        input_output_aliases=aliases,
        compiler_params=_params(2),
        name=name,
    )(p3, p3, p3, *extra_args)


def _attn_sample_kernel(q_ref, k_ref, v_ref, ckt_ref, cvt_ref, ckl_ref, cvl_ref, o_ref, nk_ref, nv_ref,
                        *, past, new):
    n_blocks = (past + new + KEY_BLOCK - 1) // KEY_BLOCK
    total = n_blocks * KEY_BLOCK
    last = (n_blocks - 1) * KEY_BLOCK
    m2 = _suffix_sum_matrix()
    row = lax.broadcasted_iota(jnp.int32, (new, KEY_BLOCK), 0)
    col = lax.broadcasted_iota(jnp.int32, (new, KEY_BLOCK), 1)
    mask = col + last < row + past
    zero_rows = jnp.zeros((total - past - new, HEAD_DIM), F32)
    nt_dims = (((1,), (1,)), ((), ()))

    q2 = q_ref[0] * Q_SCALE
    k2 = k_ref[0]
    v2 = v_ref[0]
    outs = []
    for h in range(2):
        lanes = slice(h * HEAD_DIM, (h + 1) * HEAD_DIM)
        q = q2[:, lanes].astype(BF16)
        kn = k2[:, lanes]
        vn = v2[:, lanes]
        nk_ref[0, h] = kn
        nv_ref[0, h] = vn
        k_last = jnp.concatenate([ckl_ref[0, 0, h], kn, zero_rows], axis=0).astype(BF16)
        v_last = jnp.concatenate([cvl_ref[0, 0, h], vn, zero_rows], axis=0).astype(BF16)
        z = jnp.concatenate(
            [jnp.dot(q, ckt_ref[0, 0, h].astype(BF16), preferred_element_type=F32),
             lax.dot_general(q, k_last, nt_dims, preferred_element_type=F32)], axis=1)
        lse = jnp.log(1.0 + jnp.exp(-jnp.abs(z)))
        log_beta = jnp.minimum(z, 0.0) - lse
        log_1m = log_beta - z
        blocks = [log_1m[:, j * KEY_BLOCK:(j + 1) * KEY_BLOCK] for j in range(n_blocks)]
        blocks[-1] = jnp.where(mask, blocks[-1], 0.0)
        stacked = jnp.concatenate(blocks, axis=0)
        hi = stacked.astype(BF16)
        lo = (stacked - hi.astype(F32)).astype(BF16)
        c = jnp.dot(jnp.concatenate([hi, lo], axis=1), m2, preferred_element_type=F32)
        later = jnp.zeros((new, KEY_BLOCK), F32)
        between = [None] * n_blocks
        for j in reversed(range(n_blocks)):
            cj = c[j * new:(j + 1) * new]
            between[j] = cj[:, :KEY_BLOCK] + later
            later = later + cj[:, KEY_BLOCK:]
        a = jnp.exp(log_beta + jnp.concatenate(between, axis=1))
        a_last = jnp.where(mask, a[:, last:], 0.0).astype(BF16)
        out = lax.dot_general(a[:, :last].astype(BF16), cvt_ref[0, 0, h].astype(BF16), nt_dims,
                              preferred_element_type=F32)
        outs.append(out + jnp.dot(a_last, v_last, preferred_element_type=F32))
    o_ref[0] = jnp.concatenate(outs, axis=1)


def _attn_sample(p3, cache_t, cache_last, layer, name):
    b, new, _ = p3.shape
    past = cache_t[0].shape[4]
    last = past // KEY_BLOCK * KEY_BLOCK
    assert 0 < past - last and past - last + new <= KEY_BLOCK
    col0 = 4 * A_WIDTH // LANES
    per = D_MODEL // LANES

    def col_spec(which):
        return pl.BlockSpec((1, new, LANES), lambda i, j: (i, 0, col0 + which * per + j))

    cache_spec = pl.BlockSpec((1, 1, 2, HEAD_DIM, last), lambda i, j: (layer, i, j, 0, 0))
    last_spec = pl.BlockSpec((1, 1, 2, past - last, HEAD_DIM), lambda i, j: (layer, i, j, 0, 0))
    kv_shape = jax.ShapeDtypeStruct((b, HEADS, new, HEAD_DIM), F32)
    kv_spec = pl.BlockSpec((1, 2, new, HEAD_DIM), lambda i, j: (i, j, 0, 0))
    return pl.pallas_call(
        functools.partial(_attn_sample_kernel, past=past, new=new),
        grid=(b, HEADS // 2),
        in_specs=[col_spec(0), col_spec(1), col_spec(2), cache_spec, cache_spec, last_spec, last_spec],
        out_specs=[pl.BlockSpec((1, new, LANES), lambda i, j: (i, 0, j)), kv_spec, kv_spec],
        out_shape=[jax.ShapeDtypeStruct((b, new, D_MODEL), F32), kv_shape, kv_shape],
        compiler_params=_params(2),
        name=name,
    )(p3, p3, p3, *cache_t, *cache_last)


def _sigmoid(x):
    return 0.5 * jnp.tanh(0.5 * x) + 0.5


def _silu(x):
    return x * _sigmoid(x)


def _layer_norm(x, g, b):
    mu = jnp.mean(x, axis=-1, keepdims=True)
    d = x - mu
    var = jnp.mean(d * d, axis=-1, keepdims=True)
    return d * lax.rsqrt(var + LN_EPS) * g + b


def _even_mix_kernel(x_ref, h_ref, gb_ref, gc_ref, za_ref, zb_ref, o_ref, buf_ref, cw_ref, wo_ref,
                     g_ref, b_ref, y_ref, nbuf_ref, ext, *, tm):
    hist = A_CONV_W - 1
    pad = 8

    @pl.when(pl.program_id(1) == 0)
    def _():
        ext[pad - hist:pad, :] = buf_ref[0]

    @pl.when(pl.program_id(1) > 0)
    def _():
        ext[0:pad, :] = ext[tm:tm + pad, :]

    u = gc_ref[0] * h_ref[0]
    ext[pad:pad + tm, :] = u
    cw = cw_ref[...]
    conv = cw[2:3, :] * u
    for tap in range(hist):
        conv = conv + cw[tap:tap + 1, :] * ext[pad - hist + tap:pad - hist + tap + tm, :]
    nbuf_ref[0] = ext[tm + pad - hist:tm + pad, :]
    y_a = gb_ref[0] * conv * _silu(za_ref[0])
    y_b = o_ref[0] * _silu(zb_ref[0])
    out = jnp.dot(y_a.astype(BF16), wo_ref[0:A_WIDTH, :], preferred_element_type=F32)
    out = out + jnp.dot(y_b.astype(BF16), wo_ref[A_WIDTH:, :], preferred_element_type=F32)
    y_ref[0] = _layer_norm(ALPHA * x_ref[0] + out, g_ref[...], b_ref[...])


def _even_mix(x, p3, o, conv_buf, conv_w, w_out_bf16, ln_g, ln_b, tm, name):
    b, t, d = x.shape
    hist = A_CONV_W - 1

    def row_spec(col):
        return pl.BlockSpec((1, tm, d), lambda i, j: (i, j, col))

    def full_spec(shape):
        return pl.BlockSpec(shape, lambda i, j: (0,) * len(shape))

    buf_spec = pl.BlockSpec((1, hist, A_WIDTH), lambda i, j: (i, 0, 0))
    return pl.pallas_call(
        functools.partial(_even_mix_kernel, tm=tm),
        grid=(b, t // tm),
        in_specs=[row_spec(0), row_spec(0), row_spec(1), row_spec(2), row_spec(3), row_spec(7),
                  row_spec(0), buf_spec, full_spec((A_CONV_W, A_WIDTH)),
                  full_spec((2 * D_MODEL, D_MODEL)), full_spec((1, d)), full_spec((1, d))],
        out_specs=[row_spec(0), buf_spec],
        out_shape=[jax.ShapeDtypeStruct((b, t, d), F32),
                   jax.ShapeDtypeStruct((b, hist, A_WIDTH), F32)],
        scratch_shapes=[pltpu.VMEM((tm + 8, A_WIDTH), F32)],
        compiler_params=_params(2),
        name=name,
    )(x, p3, p3, p3, p3, p3, o, conv_buf, conv_w, w_out_bf16, ln_g[None], ln_b[None])


def _odd_mix_kernel(x_ref, a_ref, gate_ref, zc_ref, buf_ref, cw_ref, cb_ref, cg_ref, cbeta_ref, wo_ref,
                    g_ref, b_ref, y_ref, nbuf_ref, ext, conv_s, shifted, *, tm):
    hist = C_CONV_W - 1
    pad = 32

    @pl.when(pl.program_id(1) == 0)
    def _():
        ext[pad - hist:pad, :] = buf_ref[0]

    @pl.when(pl.program_id(1) > 0)
    def _():
        ext[0:pad, :] = ext[tm:tm + pad, :]

    ext[pad:pad + tm, :] = a_ref[0] * _sigmoid(gate_ref[0])
    nbuf_ref[0] = ext[tm + pad - hist:tm + pad, :]

    sub = 8

    def lane_chunk(c, carry):
        lanes = pl.ds(pl.multiple_of(c * LANES, LANES), LANES)
        for phase in range(sub):
            span = tm + pad if phase == 0 else tm + pad - sub
            shifted[phase, 0:span, :] = ext[phase:phase + span, lanes]
        acc = jnp.zeros((tm, LANES), F32) + cb_ref[:, lanes]
        for tap in range(C_CONV_W):
            first_row = pad - hist + tap
            phase = first_row % sub
            acc = acc + cw_ref[tap:tap + 1, lanes] * shifted[phase, first_row - phase:first_row - phase + tm, :]
        conv_s[:, lanes] = acc
        return carry

    lax.fori_loop(0, C_WIDTH // LANES, lane_chunk, 0)
    y = _silu(_layer_norm(conv_s[...], cg_ref[...], cbeta_ref[...])) * _silu(zc_ref[0])
    out = jnp.dot(y.astype(BF16), wo_ref[...], preferred_element_type=F32)
    y_ref[0] = _layer_norm(ALPHA * x_ref[0] + out, g_ref[...], b_ref[...])


def _odd_mix(x, p3, conv_buf, conv_w, conv_b, ln_c_g, ln_c_b, w_out_bf16, ln_g, ln_b, tm, name):
    b, t, d = x.shape
    hist = C_CONV_W - 1

    def row_spec(width, col):
        return pl.BlockSpec((1, tm, width), lambda i, j: (i, j, col))

    def full_spec(shape):
        return pl.BlockSpec(shape, lambda i, j: (0,) * len(shape))

    buf_spec = pl.BlockSpec((1, hist, C_WIDTH), lambda i, j: (i, 0, 0))
    return pl.pallas_call(
        functools.partial(_odd_mix_kernel, tm=tm),
        grid=(b, t // tm),
        in_specs=[row_spec(d, 0), row_spec(C_WIDTH, 0), row_spec(C_WIDTH, 1), row_spec(C_WIDTH, 2),
                  buf_spec, full_spec((C_CONV_W, C_WIDTH)), full_spec((1, C_WIDTH)),
                  full_spec((1, C_WIDTH)), full_spec((1, C_WIDTH)), full_spec((C_WIDTH, D_MODEL)),
                  full_spec((1, d)), full_spec((1, d))],
        out_specs=[row_spec(d, 0), buf_spec],
        out_shape=[jax.ShapeDtypeStruct((b, t, d), F32),
                   jax.ShapeDtypeStruct((b, hist, C_WIDTH), F32)],
        scratch_shapes=[pltpu.VMEM((tm + 32, C_WIDTH), F32), pltpu.VMEM((tm, C_WIDTH), F32),
                        pltpu.VMEM((8, tm + 32, LANES), F32)],
        compiler_params=_params(2),
        name=name,
    )(x, p3, p3, p3, conv_buf, conv_w, conv_b[None], ln_c_g[None], ln_c_b[None], w_out_bf16,
      ln_g[None], ln_b[None])


PROMPT_ROW_TILE = 344
PROJ_COL_TILE = 1024


def _trunk(x, conv_a_bufs, conv_c_bufs, cache_k, cache_v, weights, tag):
    (w_in_even, conv_a_w, w_out_even, w_in_odd, conv_c_w, conv_c_b, ln_c_g, ln_c_b, w_out_odd,
     post_ln_g, post_ln_b) = weights
    b, t, d = x.shape
    prompt = cache_k is None
    if not prompt:
        whole = cache_k.shape[3] // KEY_BLOCK * KEY_BLOCK
        cache_t = (jnp.swapaxes(cache_k, 3, 4), jnp.swapaxes(cache_v, 3, 4))
        cache_last = (cache_k[:, :, :, whole:, :], cache_v[:, :, :, whole:, :])
    proj_tm = t if prompt else b * t
    mix_tm = PROMPT_ROW_TILE if prompt else t
    n_even = w_in_even.shape[0]
    new_k, new_v, new_a, new_c = [], [], [], []
    stacked_kv = None
    for layer in range(DEPTH):
        i = layer // 2
        nm = f"{tag}{layer}"
        if layer % 2 == 0:
            p3 = _proj(x.reshape(b * t, d), w_in_even[i], proj_tm, PROJ_COL_TILE,
                       f"proj_{nm}").reshape(b, t, EVEN_IN)
            if prompt:
                o, *stacked_kv = _attn_prompt(p3, i, n_even, stacked_kv, f"attn_{nm}")
            else:
                o, kh, vh = _attn_sample(p3, cache_t, cache_last, i, f"attn_{nm}")
                new_k.append(kh)
                new_v.append(vh)
            x, buf = _even_mix(x, p3, o, conv_a_bufs[i], conv_a_w[i], w_out_even[i],
                               post_ln_g[layer], post_ln_b[layer], mix_tm, f"mix_{nm}")
            new_a.append(buf)
        else:
            p3 = _proj(x.reshape(b * t, d), w_in_odd[i], proj_tm, PROJ_COL_TILE,
                       f"proj_{nm}").reshape(b, t, ODD_IN)
            x, buf = _odd_mix(x, p3, conv_c_bufs[i], conv_c_w[i], conv_c_b[i], ln_c_g[i], ln_c_b[i],
                              w_out_odd[i], post_ln_g[layer], post_ln_b[layer], mix_tm, f"mix_{nm}")
            new_c.append(buf)
    if prompt:
        k_all, v_all = (jnp.swapaxes(kv, 3, 4) for kv in stacked_kv)
    else:
        k_all, v_all = jnp.stack(new_k), jnp.stack(new_v)
    return x, k_all, v_all, jnp.stack(new_a), jnp.stack(new_c)


def kernel(x_prompt, x_sample, cache_sb_k, cache_sb_v, state_conv_a, state_conv_c, meta_tokens,
           w_in_even, conv_a_w, w_out_even, w_in_odd, conv_c_w, conv_c_b, ln_c_g, ln_c_b,
           w_out_odd, post_ln_g, post_ln_b):
    weights = (w_in_even.astype(BF16), conv_a_w, w_out_even.astype(BF16), w_in_odd.astype(BF16),
               conv_c_w, conv_c_b, ln_c_g, ln_c_b, w_out_odd.astype(BF16), post_ln_g, post_ln_b)
    b_p = x_prompt.shape[0]
    n_even = w_in_even.shape[0]
    n_odd = w_in_odd.shape[0]
    meta = jnp.broadcast_to(meta_tokens.astype(x_prompt.dtype)[None], (b_p, N_META, D_MODEL))
    xp = jnp.concatenate([meta, x_prompt], axis=1)
    zeros_a = jnp.zeros((n_even, b_p, A_CONV_W - 1, A_WIDTH), x_prompt.dtype)
    zeros_c = jnp.zeros((n_odd, b_p, C_CONV_W - 1, C_WIDTH), x_prompt.dtype)
    h_p, k_p, v_p, a_p, c_p = _trunk(xp, zeros_a, zeros_c, None, None, weights, "p")
    y_sample, k_s, v_s, a_s, c_s = _trunk(x_sample, state_conv_a, state_conv_c, cache_sb_k,
                                          cache_sb_v, weights, "s")
    return (h_p[:, N_META:], y_sample, k_p, v_p, a_p, c_p, k_s, v_s, a_s, c_s)
```
